```python
import math, functools
import jax, jax.numpy as jnp
from jax import lax
import numpy as np

D_MODEL = 2048
BATCH = 2
SEQ = 4096
DEPTH = 2
DEC_BATCH = 128
DEC_SEQ = 8
PAST_LEN = 2048
PAGE_SIZE = 128

MIX_WIDTH = D_MODEL
DA_HEADS = 8
DA_HEAD_DIM = MIX_WIDTH // (4 * DA_HEADS)
DA_VDIM = 2 * DA_HEAD_DIM
DA_WIDTH = DA_HEADS * DA_VDIM
QK_WIDTH = DA_HEADS * 2 * DA_HEAD_DIM
GM_WIDTH = MIX_WIDTH - DA_WIDTH
GM_HEADS = 8
GM_DIM = GM_WIDTH // GM_HEADS
CHUNK = 128
IN_WIDTH = 2 * QK_WIDTH + DA_WIDTH + 2 * GM_WIDTH
F_DENSE = 256 * ((8 * D_MODEL // 3 + 255) // 256)
N_EXPERTS = 8
TOP_K = 2
F_EXPERT = 7 * D_MODEL // 2
N_DENSE = (DEPTH + 1) // 2
N_MOE = DEPTH // 2
Q_BLOCK = 128
EPS = 1e-6

kernel_name = 'hybrid_diffattn_chunkmlp_decoder_step'


def rmsnorm(x, g):
    xf = x.astype(jnp.float32)
    y = xf * lax.rsqrt(jnp.mean(xf * xf, axis=-1, keepdims=True) + EPS)
    return (y * g.astype(jnp.float32)).astype(x.dtype)


def layernorm(x, g, b):
    xf = x.astype(jnp.float32)
    mu = jnp.mean(xf, axis=-1, keepdims=True)
    var = jnp.mean(jnp.square(xf - mu), axis=-1, keepdims=True)
    y = (xf - mu) * lax.rsqrt(var + EPS) * g.astype(jnp.float32) + b.astype(jnp.float32)
    return y.astype(x.dtype)


def alibi_slopes():
    return jnp.exp2(-8.0 * jnp.arange(1, DA_HEADS + 1, dtype=jnp.float32) / DA_HEADS)


def adaln(c, w_ada, b_ada):
    mod = jnp.dot(jax.nn.silu(c), w_ada) + b_ada
    return jnp.split(mod[:, None, :], 6, axis=-1)


def split_in(z):
    lead = z.shape[:-1]
    o = 0
    q = z[..., o:o + QK_WIDTH].reshape(*lead, DA_HEADS, 2, DA_HEAD_DIM); o += QK_WIDTH
    k = z[..., o:o + QK_WIDTH].reshape(*lead, DA_HEADS, 2, DA_HEAD_DIM); o += QK_WIDTH
    v = z[..., o:o + DA_WIDTH].reshape(*lead, DA_HEADS, DA_VDIM); o += DA_WIDTH
    gu = z[..., o:o + GM_WIDTH].reshape(*lead, GM_HEADS, GM_DIM); o += GM_WIDTH
    gv = z[..., o:o + GM_WIDTH].reshape(*lead, GM_HEADS, GM_DIM)
    return q, k, v, gu, gv


def diff_attn_prompt(q, k, v, lam):
    B, S = q.shape[:2]
    nb = S // Q_BLOCK
    scale = DA_HEAD_DIM ** -0.5
    slopes = alibi_slopes()[:, None, None, None]
    qb = jnp.moveaxis(q.reshape(B, nb, Q_BLOCK, DA_HEADS, 2, DA_HEAD_DIM), 1, 0)
    kpos = jnp.arange(S)

    def one_block(args):
        qi, i = args
        qpos = i * Q_BLOCK + jnp.arange(Q_BLOCK)
        s = jnp.einsum('bqhmd,bkhmd->bhmqk', qi, k, preferred_element_type=jnp.float32) * scale
        dist = (qpos[:, None] - kpos[None, :]).astype(jnp.float32)
        s = jnp.where(dist >= 0, s - slopes * dist, -jnp.inf)
        p = jax.nn.softmax(s, axis=-1)
        a = p[:, :, 0] - lam * p[:, :, 1]
        return jnp.einsum('bhqk,bkhe->bqhe', a.astype(v.dtype), v)

    o = lax.map(one_block, (qb, jnp.arange(nb)))
    return jnp.moveaxis(o, 0, 1).reshape(B, S, DA_HEADS, DA_VDIM)


def diff_attn_sample(q, k_new, v_new, lam, k_past, v_past):
    T = q.shape[1]
    P = k_past.shape[1]
    scale = DA_HEAD_DIM ** -0.5
    slopes = alibi_slopes()[:, None, None, None]
    qpos = P + jnp.arange(T)
    dist_past = (qpos[:, None] - jnp.arange(P)[None, :]).astype(jnp.float32)
    dist_new = (jnp.arange(T)[:, None] - jnp.arange(T)[None, :]).astype(jnp.float32)
    s_past = jnp.einsum('bqhmd,bkhmd->bhmqk', q, k_past, preferred_element_type=jnp.float32) * scale
    s_new = jnp.einsum('bqhmd,bkhmd->bhmqk', q, k_new, preferred_element_type=jnp.float32) * scale
    s_past = s_past - slopes * dist_past
    s_new = jnp.where(dist_new >= 0, s_new - slopes * dist_new, -jnp.inf)
    p = jax.nn.softmax(jnp.concatenate([s_past, s_new], axis=-1), axis=-1)
    a = (p[:, :, 0] - lam * p[:, :, 1]).astype(v_new.dtype)
    return (jnp.einsum('bhqk,bkhe->bqhe', a[..., :P], v_past)
            + jnp.einsum('bhqk,bkhe->bqhe', a[..., P:], v_new))


def chunk_mlp(u, v, ws, bias, ng, nb, chunk_len):
    B, S = u.shape[:2]
    u = jax.nn.gelu(u)
    v = layernorm(jax.nn.gelu(v), ng, nb)
    w = jnp.tril(ws[:, :chunk_len, :chunk_len])
    vc = v.reshape(B, S // chunk_len, chunk_len, GM_HEADS, GM_DIM)
    mixed = jnp.einsum('gts,bnsgc->bntgc', w, vc) + bias[:, :chunk_len].T[None, None, :, :, None]
    return u * mixed.reshape(B, S, GM_HEADS, GM_DIM), v


def swiglu(x, w_gate, w_up, w_down):
    return jnp.dot(jax.nn.silu(jnp.dot(x, w_gate)) * jnp.dot(x, w_up), w_down)


def moe_swiglu(x, w_router, w_gate, w_up, w_down):
    xf = x.reshape(-1, x.shape[-1])
    logits = jnp.dot(xf, w_router).astype(jnp.float32)
    top_v, top_i = lax.top_k(logits, TOP_K)
    gates = jax.nn.softmax(top_v, axis=-1)
    dense_gate = jnp.sum(jax.nn.one_hot(top_i, N_EXPERTS, dtype=jnp.float32) * gates[..., None], axis=-2)
    dense_gate = dense_gate.astype(xf.dtype)
    y = jnp.zeros_like(xf)
    for e in range(N_EXPERTS):
        y = y + dense_gate[:, e:e + 1] * swiglu(xf, w_gate[e], w_up[e], w_down[e])
    return y.reshape(x.shape)


def mix_sublayer(h, mix_params, lam, lam_init, attend, chunk_len):
    w_in, w_out, g_subln, gm_ws, gm_b, gm_ng, gm_nb = mix_params
    z = jnp.dot(h, w_in)
    q, k, v, gu, gv = split_in(z)
    o_att = attend(q, k, v, lam)
    o_att = rmsnorm(o_att, g_subln) * (1.0 - lam_init)
    o_gm, gv_rows = chunk_mlp(gu, gv, gm_ws, gm_b, gm_ng, gm_nb, chunk_len)
    lead = h.shape[:-1]
    o = jnp.concatenate([o_att.reshape(*lead, DA_WIDTH), o_gm.reshape(*lead, GM_WIDTH)], axis=-1)
    return jnp.dot(o, w_out), k, v, gv_rows


def layer_forward(x, c, norms, ada, mix_params, lam, lam_init, attend, chunk_len, ffn):
    g_pre_m, g_post_m, g_pre_f, g_post_f = norms
    sh_m, sc_m, gt_m, sh_f, sc_f, gt_f = adaln(c, ada[0], ada[1])
    h = rmsnorm(x, g_pre_m) * (1.0 + sc_m) + sh_m
    m, k_rows, v_rows, gv_rows = mix_sublayer(h, mix_params, lam, lam_init, attend, chunk_len)
    x = x + gt_m * rmsnorm(m, g_post_m)
    h = rmsnorm(x, g_pre_f) * (1.0 + sc_f) + sh_f
    x = x + gt_f * rmsnorm(ffn(h), g_post_f)
    return x, k_rows, v_rows, gv_rows


def setup_inputs(seed: int = 0) -> dict:
    key = jax.random.key(seed)
    ks = jax.random.split(key, 40)
    n_pages = PAST_LEN // PAGE_SIZE
    n_used = DEC_BATCH * n_pages
    n_pool = n_used + (n_used + 3) // 4
    nrm = lambda k, shape, s: jax.random.normal(k, shape, jnp.float32) * s
    gain = lambda k, shape: 1.0 + 0.02 * jax.random.normal(k, shape, jnp.float32)
    page_table = jax.random.permutation(ks[6], n_pool)[:n_used].reshape(DEC_BATCH, n_pages).astype(jnp.int32)
    return {
        'x_prompt': nrm(ks[0], (BATCH, SEQ, D_MODEL), 1.0),
        'x_sample': nrm(ks[1], (DEC_BATCH, DEC_SEQ, D_MODEL), 1.0),
        'c_prompt': nrm(ks[2], (BATCH, D_MODEL), 1.0),
        'c_sample': nrm(ks[3], (DEC_BATCH, D_MODEL), 1.0),
        'cache_k': nrm(ks[4], (DEPTH, n_pool, PAGE_SIZE, DA_HEADS, 2, DA_HEAD_DIM), 1.0),
        'cache_v': nrm(ks[5], (DEPTH, n_pool, PAGE_SIZE, DA_HEADS, DA_VDIM), 1.0),
        'page_table': page_table,
        'g_pre_mix': gain(ks[7], (DEPTH, D_MODEL)),
        'g_post_mix': gain(ks[8], (DEPTH, D_MODEL)),
        'g_pre_ffn': gain(ks[9], (DEPTH, D_MODEL)),
        'g_post_ffn': gain(ks[10], (DEPTH, D_MODEL)),
        'w_ada': nrm(ks[11], (DEPTH, D_MODEL, 6 * D_MODEL), D_MODEL ** -0.5),
        'b_ada': nrm(ks[12], (DEPTH, 6 * D_MODEL), 0.02),
        'w_in': nrm(ks[13], (DEPTH, D_MODEL, IN_WIDTH), D_MODEL ** -0.5),
        'w_out': nrm(ks[14], (DEPTH, MIX_WIDTH, D_MODEL), MIX_WIDTH ** -0.5),
        'lam_q1': nrm(ks[15], (DEPTH, DA_HEAD_DIM), 0.1),
        'lam_k1': nrm(ks[16], (DEPTH, DA_HEAD_DIM), 0.1),
        'lam_q2': nrm(ks[17], (DEPTH, DA_HEAD_DIM), 0.1),
        'lam_k2': nrm(ks[18], (DEPTH, DA_HEAD_DIM), 0.1),
        'g_subln': gain(ks[19], (DEPTH, DA_VDIM)),
        'gm_ws': nrm(ks[20], (DEPTH, GM_HEADS, CHUNK, CHUNK), CHUNK ** -0.5),
        'gm_b': gain(ks[21], (DEPTH, GM_HEADS, CHUNK)),
        'gm_norm_g': gain(ks[22], (DEPTH, GM_HEADS, GM_DIM)),
        'gm_norm_b': nrm(ks[23], (DEPTH, GM_HEADS, GM_DIM), 0.02),
        'w_d_gate': nrm(ks[24], (N_DENSE, D_MODEL, F_DENSE), D_MODEL ** -0.5),
        'w_d_up': nrm(ks[25], (N_DENSE, D_MODEL, F_DENSE), D_MODEL ** -0.5),
        'w_d_down': nrm(ks[26], (N_DENSE, F_DENSE, D_MODEL), F_DENSE ** -0.5),
        'w_router': nrm(ks[27], (N_MOE, D_MODEL, N_EXPERTS), D_MODEL ** -0.5),
        'w_e_gate': nrm(ks[28], (N_MOE, N_EXPERTS, D_MODEL, F_EXPERT), D_MODEL ** -0.5),
        'w_e_up': nrm(ks[29], (N_MOE, N_EXPERTS, D_MODEL, F_EXPERT), D_MODEL ** -0.5),
        'w_e_down': nrm(ks[30], (N_MOE, N_EXPERTS, F_EXPERT, D_MODEL), F_EXPERT ** -0.5),
    }


def reference(x_prompt, x_sample, c_prompt, c_sample, cache_k, cache_v, page_table,
              g_pre_mix, g_post_mix, g_pre_ffn, g_post_ffn, w_ada, b_ada, w_in, w_out,
              lam_q1, lam_k1, lam_q2, lam_k2, g_subln, gm_ws, gm_b, gm_norm_g, gm_norm_b,
              w_d_gate, w_d_up, w_d_down, w_router, w_e_gate, w_e_up, w_e_down):
    n_dec = page_table.shape[0]
    past = page_table.shape[1] * PAGE_SIZE
    dec_len = x_sample.shape[1]
    xp, xs = x_prompt, x_sample
    k_p, v_p, k_s, v_s, gv_s = [], [], [], [], []
    for l in range(DEPTH):
        lam_init = 0.8 - 0.6 * math.exp(-0.3 * l)
        lam = (jnp.exp(jnp.sum(lam_q1[l].astype(jnp.float32) * lam_k1[l].astype(jnp.float32)))
               - jnp.exp(jnp.sum(lam_q2[l].astype(jnp.float32) * lam_k2[l].astype(jnp.float32)))
               + lam_init)
        norms = (g_pre_mix[l], g_post_mix[l], g_pre_ffn[l], g_post_ffn[l])
        ada = (w_ada[l], b_ada[l])
        mix_params = (w_in[l], w_out[l], g_subln[l], gm_ws[l], gm_b[l], gm_norm_g[l], gm_norm_b[l])
        i = l // 2
        if l % 2 == 0:
            ffn = functools.partial(swiglu, w_gate=w_d_gate[i], w_up=w_d_up[i], w_down=w_d_down[i])
        else:
            ffn = functools.partial(moe_swiglu, w_router=w_router[i], w_gate=w_e_gate[i],
                                    w_up=w_e_up[i], w_down=w_e_down[i])
        xp, kr, vr, _ = layer_forward(xp, c_prompt, norms, ada, mix_params, lam, lam_init,
                                      diff_attn_prompt, CHUNK, ffn)
        kp = cache_k[l, page_table].reshape(n_dec, past, DA_HEADS, 2, DA_HEAD_DIM)
        vp = cache_v[l, page_table].reshape(n_dec, past, DA_HEADS, DA_VDIM)
        attend_s = functools.partial(diff_attn_sample, k_past=kp, v_past=vp)
        xs, kr2, vr2, gvr = layer_forward(xs, c_sample, norms, ada, mix_params, lam, lam_init,
                                          attend_s, dec_len, ffn)
        k_p.append(kr); v_p.append(vr); k_s.append(kr2); v_s.append(vr2); gv_s.append(gvr)
    k_prompt = jnp.stack(k_p)
    v_prompt = jnp.stack(v_p)
    k_sample = jnp.stack(k_s)
    v_sample = jnp.stack(v_s)
    gv_sample = jnp.stack(gv_s)
    return (xp, xs, k_prompt, v_prompt, k_sample, v_sample, gv_sample)
```

```python
import functools
import math

import jax
import jax.numpy as jnp
from jax import lax
from jax.experimental import pallas as pl
from jax.experimental.pallas import tpu as pltpu

F32 = jnp.float32
BF16 = jnp.bfloat16
I32 = jnp.int32

EPS = 1e-6
DA_HEADS = 8
DA_HEAD_DIM = 64
DA_VDIM = 128
GM_HEADS = 8
GM_DIM = 128
CHUNK = 128
PAGE_SIZE = 128
N_EXPERTS = 8
NEG = -1e30

V7X_VMEM_BYTES = 64 * 1024 * 1024
VMEM_LIMIT = V7X_VMEM_BYTES - 8 * 1024 * 1024

NT_DIMS = (((1,), (1,)), ((), ()))


def _cparams(sem):
    return pltpu.CompilerParams(dimension_semantics=sem, vmem_limit_bytes=VMEM_LIMIT)


def _fit_tile(n, t):
    while n % t:
        t //= 2
    return t


def _rms(x, g):
    return x * lax.rsqrt(jnp.mean(x * x, axis=-1, keepdims=True) + EPS) * g


def _ada_kernel(c_ref, w_ref, b_ref, o_ref):
    c = c_ref[...]
    a = (c * jax.nn.sigmoid(c)).astype(BF16)
    o_ref[...] = jnp.dot(a, w_ref[...].astype(BF16), preferred_element_type=F32) + b_ref[...]


def ada_call(c_all, w_ada, b_ada):
    R, D = c_all.shape
    L, _, N = w_ada.shape
    tn = min(1024, N)
    return pl.pallas_call(
        _ada_kernel,
        out_shape=jax.ShapeDtypeStruct((L, R, N), F32),
        grid=(L, N // tn),
        in_specs=[
            pl.BlockSpec((R, D), lambda l, j: (0, 0)),
            pl.BlockSpec((None, D, tn), lambda l, j: (l, 0, j)),
            pl.BlockSpec((None, 1, tn), lambda l, j: (l, 0, j)),
        ],
        out_specs=pl.BlockSpec((None, R, tn), lambda l, j: (l, 0, j)),
        compiler_params=_cparams(("arbitrary", "arbitrary")),
        name="ada",
    )(c_all, w_ada, b_ada.reshape(L, 1, N))


class RowGeom:
    def __init__(self, n_batch, seq, n_dec, dec_len, tm):
        assert seq % tm == 0 and (n_dec * dec_len) % tm == 0
        self.tm = tm
        self.n_batch = n_batch
        self.tiles_per_batch = seq // tm
        self.n_ptiles = n_batch * self.tiles_per_batch
        self.n_tiles = self.n_ptiles + (n_dec * dec_len) // tm


def _mod_specs(geom, D, l, col):
    tpb, npt, nb = geom.tiles_per_batch, geom.n_ptiles, geom.n_batch
    sp = pl.BlockSpec((None, None, 1, D),
                      lambda i, *_: (l, jnp.minimum(i // tpb, nb - 1), 0, col))
    ss = pl.BlockSpec((None, geom.tm, D),
                      lambda i, *_: (l, jnp.maximum(i - npt, 0), col))
    return [sp, ss]


def _mod_value(i, n_ptiles, p_ref, s_ref):
    return jnp.where(i < n_ptiles, p_ref[...], s_ref[...])


def _norm_kernel(*refs, n_ptiles, post, pre):
    it = iter(refs)
    x_ref = next(it)
    if post:
        m_ref, gpost_ref, gtp, gts = next(it), next(it), next(it), next(it)
    if pre:
        gpre_ref, scp, scs, shp, shs = next(it), next(it), next(it), next(it), next(it)
    if post:
        xo_ref = next(it)
    if pre:
        h_ref = next(it)
    i = pl.program_id(0)
    x = x_ref[...]
    if post:
        x = x + _mod_value(i, n_ptiles, gtp, gts) * _rms(m_ref[...], gpost_ref[...])
        xo_ref[...] = x
    if pre:
        h = _rms(x, gpre_ref[...]) * (1.0 + _mod_value(i, n_ptiles, scp, scs)) \
            + _mod_value(i, n_ptiles, shp, shs)
        h_ref[...] = h.astype(h_ref.dtype)


def norm_call(geom, x, modp, mods, *, m=None, g_post=None, l_post=None, gt_col=None,
              g_pre=None, l_pre=None, sc_col=None, sh_col=None, h_dtype=BF16):
    T, D = x.shape
    tm = geom.tm
    post, pre = m is not None, g_pre is not None
    row = pl.BlockSpec((tm, D), lambda i: (i, 0))
    args, specs = [x], [row]
    if post:
        args += [m, g_post, modp, mods]
        specs += [row, pl.BlockSpec((None, 1, D), lambda i: (l_post, 0, 0))]
        specs += _mod_specs(geom, D, l_post, gt_col)
    if pre:
        args += [g_pre, modp, mods, modp, mods]
        specs += [pl.BlockSpec((None, 1, D), lambda i: (l_pre, 0, 0))]
        specs += _mod_specs(geom, D, l_pre, sc_col) + _mod_specs(geom, D, l_pre, sh_col)
    out_shape, out_specs = [], []
    if post:
        out_shape.append(jax.ShapeDtypeStruct((T, D), F32))
        out_specs.append(row)
    if pre:
        out_shape.append(jax.ShapeDtypeStruct((T, D), h_dtype))
        out_specs.append(row)
    outs = pl.pallas_call(
        functools.partial(_norm_kernel, n_ptiles=geom.n_ptiles, post=post, pre=pre),
        out_shape=out_shape, grid=(geom.n_tiles,), in_specs=specs, out_specs=out_specs,
        compiler_params=_cparams(("arbitrary",)), name="norm",
    )(*args)
    return outs


def _mm_kernel(*refs, n_a):
    o_ref = refs[2 * n_a]
    acc = None
    for a_ref, w_ref in zip(refs[:n_a], refs[n_a:2 * n_a]):
        d = jnp.dot(a_ref[...].astype(BF16), w_ref[...].astype(BF16), preferred_element_type=F32)
        acc = d if acc is None else acc + d
    o_ref[...] = acc


def mm_call(a_list, w, l, *, tm=1024, tn=512):
    T = a_list[0].shape[0]
    N = w.shape[2]
    tm, tn = _fit_tile(T, tm), _fit_tile(N, tn)
    specs = [pl.BlockSpec((tm, a.shape[1]), lambda i, j: (i, 0)) for a in a_list]
    for k, a in enumerate(a_list):
        specs.append(pl.BlockSpec((None, a.shape[1], tn), lambda i, j, k=k: (l, k, j)))
    return pl.pallas_call(
        functools.partial(_mm_kernel, n_a=len(a_list)),
        out_shape=jax.ShapeDtypeStruct((T, N), F32),
        grid=(T // tm, N // tn), in_specs=specs,
        out_specs=pl.BlockSpec((tm, tn), lambda i, j: (i, j)),
        compiler_params=_cparams(("arbitrary", "arbitrary")), name="matmul",
    )(*a_list, *([w] * len(a_list)))


def _lam_value(lamv_ref, lam_init):
    v = lamv_ref[...]
    s1 = jnp.sum(v[0:1] * v[1:2], axis=-1, keepdims=True)
    s2 = jnp.sum(v[2:3] * v[3:4], axis=-1, keepdims=True)
    return jnp.exp(s1) - jnp.exp(s2) + lam_init


def _alibi_slope(head):
    return lax.bitcast_convert_type((126 - head) << 23, F32)


def _softmax_step(s, v, m_old, l_old, acc_old):
    m_new = jnp.maximum(m_old, jnp.max(s, axis=-1, keepdims=True))
    alpha = jnp.exp(m_old - m_new)
    p = jnp.exp(s - m_new)
    l_new = alpha * l_old + jnp.sum(p, axis=-1, keepdims=True)
    acc_new = alpha * acc_old + jnp.dot(p.astype(BF16), v, preferred_element_type=F32)
    return m_new, l_new, acc_new


def _attn_p_kernel(qi_tab, ki_tab, lamv_ref, q_ref, k_ref, v_ref, g_ref, o_ref,
                   q_s, m_s, l_s, acc_s, *, tq, lam_init):
    h = pl.program_id(1)
    p = pl.program_id(2)
    qi, ki = qi_tab[p], ki_tab[p]

    @pl.when(ki == 0)
    def _():
        q = q_ref[...] * (DA_HEAD_DIM ** -0.5)
        lane = lax.broadcasted_iota(I32, q.shape, 1)
        q_s[0] = jnp.where(lane < DA_HEAD_DIM, q, 0.0).astype(BF16)
        q_s[1] = jnp.where(lane >= DA_HEAD_DIM, q, 0.0).astype(BF16)
        m_s[...] = jnp.full(m_s.shape, NEG, F32)
        l_s[...] = jnp.zeros(l_s.shape, F32)
        acc_s[...] = jnp.zeros(acc_s.shape, F32)

    def update(masked):
        k = k_ref[...].astype(BF16)
        v = v_ref[...].astype(BF16)
        col = lax.broadcasted_iota(I32, (1, tq), 1)
        slope = _alibi_slope(jnp.zeros((1, tq), I32) + h)
        bias = slope * ((ki - qi) * tq + col).astype(F32)
        for mi in range(2):
            s = lax.dot_general(q_s[mi], k, NT_DIMS, preferred_element_type=F32) + bias
            if masked:
                r = lax.broadcasted_iota(I32, (tq, tq), 0)
                c = lax.broadcasted_iota(I32, (tq, tq), 1)
                s = jnp.where(c <= r, s, NEG)
            m_s[mi], l_s[mi], acc_s[mi] = _softmax_step(s, v, m_s[mi], l_s[mi], acc_s[mi])

    @pl.when(ki < qi)
    def _():
        update(False)

    @pl.when(ki == qi)
    def _():
        update(True)
        lam = _lam_value(lamv_ref, lam_init)
        o = acc_s[0] / l_s[0] - lam * (acc_s[1] / l_s[1])
        o_ref[...] = _rms(o, g_ref[...]) * (1.0 - lam_init)


def attn_prompt_call(z, lamv, g_subln, l, *, n_batch, seq, lam_init, tq=512):
    tq = min(tq, seq)
    nq = seq // tq
    pairs = [(qi, ki) for qi in range(nq) for ki in range(qi + 1)]
    qi_tab = jnp.asarray([p[0] for p in pairs], I32)
    ki_tab = jnp.asarray([p[1] for p in pairs], I32)
    H = DA_HEADS
    w = 2 * DA_HEAD_DIM
    grid_spec = pltpu.PrefetchScalarGridSpec(
        num_scalar_prefetch=2,
        grid=(n_batch, H, len(pairs)),
        in_specs=[
            pl.BlockSpec((None, 4, DA_HEAD_DIM), lambda b, h, p, qt, kt: (l, 0, 0)),
            pl.BlockSpec((tq, w), lambda b, h, p, qt, kt: (b * nq + qt[p], h)),
            pl.BlockSpec((tq, w), lambda b, h, p, qt, kt: (b * nq + kt[p], H + h)),
            pl.BlockSpec((tq, DA_VDIM), lambda b, h, p, qt, kt: (b * nq + kt[p], 2 * H + h)),
            pl.BlockSpec((None, 1, DA_VDIM), lambda b, h, p, qt, kt: (l, 0, 0)),
        ],
        out_specs=pl.BlockSpec((tq, DA_VDIM), lambda b, h, p, qt, kt: (b * nq + qt[p], h)),
        scratch_shapes=[
            pltpu.VMEM((2, tq, w), BF16),
            pltpu.VMEM((2, tq, 1), F32),
            pltpu.VMEM((2, tq, 1), F32),
            pltpu.VMEM((2, tq, DA_VDIM), F32),
        ],
    )
    return pl.pallas_call(
        functools.partial(_attn_p_kernel, tq=tq, lam_init=lam_init),
        out_shape=jax.ShapeDtypeStruct((n_batch * seq, H * DA_VDIM), F32),
        grid_spec=grid_spec,
        compiler_params=_cparams(("arbitrary", "arbitrary", "arbitrary")),
        name="attn_prompt",
    )(qi_tab, ki_tab, lamv, z, z, z, g_subln)


def _attn_s_kernel(pt_ref, lamv_ref, q_ref, kn_ref, vn_ref, kc_ref, vc_ref, g_ref, o_ref,
                   qx_s, slope_s, m_s, l_s, acc_s, *, n_pages, dec_len, lam_init):
    j = pl.program_id(1)
    H, W = DA_HEADS, 2 * DA_HEAD_DIM * DA_HEADS
    R = 2 * H * dec_len
    past = n_pages * PAGE_SIZE

    @pl.when(j == 0)
    def _():
        q = q_ref[...] * (DA_HEAD_DIM ** -0.5)
        qt = jnp.broadcast_to(q[None], (2 * H, dec_len, W)).reshape(R, W)
        r = lax.broadcasted_iota(I32, (R, W), 0)
        c = lax.broadcasted_iota(I32, (R, W), 1)
        row_map, row_head = r // (H * dec_len), (r // dec_len) % H
        col_head, col_map = c // (2 * DA_HEAD_DIM), (c // DA_HEAD_DIM) % 2
        keep = jnp.logical_and(row_map == col_map, row_head == col_head)
        qx_s[...] = jnp.where(keep, qt, 0.0).astype(BF16)
        rr = lax.broadcasted_iota(I32, (R, 1), 0)
        slope_s[...] = _alibi_slope((rr // dec_len) % H)
        m_s[...] = jnp.full(m_s.shape, NEG, F32)
        l_s[...] = jnp.zeros(l_s.shape, F32)
        acc_s[...] = jnp.zeros(acc_s.shape, F32)

    def update(kf, vf, bias, mask):
        s = lax.dot_general(qx_s[...], kf.astype(BF16), NT_DIMS, preferred_element_type=F32) + bias
        if mask is not None:
            s = jnp.where(mask, s, NEG)
        m_s[...], l_s[...], acc_s[...] = _softmax_step(
            s, vf.astype(BF16), m_s[...], l_s[...], acc_s[...])

    lane = lax.broadcasted_iota(I32, (R, PAGE_SIZE), 1)
    update(kc_ref[...], vc_ref[...], slope_s[...] * (j * PAGE_SIZE + lane - past).astype(F32), None)

    @pl.when(j == n_pages - 1)
    def _():
        pad = jnp.zeros((PAGE_SIZE - dec_len, W), F32)
        kf = jnp.concatenate([kn_ref[...], pad], axis=0)
        vf = jnp.concatenate([vn_ref[...], pad], axis=0)
        qrow = lax.broadcasted_iota(I32, (R, PAGE_SIZE), 0) % dec_len
        update(kf, vf, slope_s[...] * lane.astype(F32), lane <= qrow)
        lam = _lam_value(lamv_ref, lam_init)
        accn = acc_s[...] / l_s[...]
        hq = H * dec_len
        for h in range(H):
            cols = slice(h * DA_VDIM, (h + 1) * DA_VDIM)
            o1 = accn[h * dec_len:(h + 1) * dec_len, cols]
            o2 = accn[hq + h * dec_len:hq + (h + 1) * dec_len, cols]
            o_ref[:, cols] = _rms(o1 - lam * o2, g_ref[...]) * (1.0 - lam_init)


def attn_sample_call(z, cache_k, cache_v, page_table, lamv, g_subln, l, *, row0, dec_len, lam_init):
    n_dec, n_pages = page_table.shape
    H = DA_HEADS
    W = 2 * DA_HEAD_DIM * H
    n_pool = cache_k.shape[1]
    ck = cache_k.reshape(cache_k.shape[0], n_pool, PAGE_SIZE, W)
    cv = cache_v.reshape(cache_v.shape[0], n_pool, PAGE_SIZE, H * DA_VDIM)
    rb = row0 // dec_len
    R = 2 * H * dec_len
    grid_spec = pltpu.PrefetchScalarGridSpec(
        num_scalar_prefetch=1,
        grid=(n_dec, n_pages),
        in_specs=[
            pl.BlockSpec((None, 4, DA_HEAD_DIM), lambda b, j, pt: (l, 0, 0)),
            pl.BlockSpec((dec_len, W), lambda b, j, pt: (rb + b, 0)),
            pl.BlockSpec((dec_len, W), lambda b, j, pt: (rb + b, 1)),
            pl.BlockSpec((dec_len, W), lambda b, j, pt: (rb + b, 2)),
            pl.BlockSpec((None, None, PAGE_SIZE, W), lambda b, j, pt: (l, pt[b * n_pages + j], 0, 0)),
            pl.BlockSpec((None, None, PAGE_SIZE, W), lambda b, j, pt: (l, pt[b * n_pages + j], 0, 0)),
            pl.BlockSpec((None, 1, DA_VDIM), lambda b, j, pt: (l, 0, 0)),
        ],
        out_specs=pl.BlockSpec((dec_len, W), lambda b, j, pt: (b, 0)),
        scratch_shapes=[
            pltpu.VMEM((R, W), BF16),
            pltpu.VMEM((R, 1), F32),
            pltpu.VMEM((R, 1), F32),
            pltpu.VMEM((R, 1), F32),
            pltpu.VMEM((R, W), F32),
        ],
    )
    return pl.pallas_call(
        functools.partial(_attn_s_kernel, n_pages=n_pages, dec_len=dec_len, lam_init=lam_init),
        out_shape=jax.ShapeDtypeStruct((n_dec * dec_len, W), F32),
        grid_spec=grid_spec,
        compiler_params=_cparams(("arbitrary", "arbitrary")),
        name="attn_sample",
    )(page_table.reshape(-1), lamv, z, z, z, ck, cv, g_subln)


def _gm_kernel(gu_ref, gv_ref, w_ref, b_ref, ng_ref, nb_ref, o_ref, gvo_ref, *, n_ptiles, dec_len):
    i = pl.program_id(0)
    r = lax.broadcasted_iota(I32, (CHUNK, CHUNK), 0)
    c = lax.broadcasted_iota(I32, (CHUNK, CHUNK), 1)
    same_seq = jnp.where(r // dec_len == c // dec_len, 1, 0) + jnp.where(i < n_ptiles, 1, 0)
    keep = jnp.logical_and(c <= r, same_seq > 0)
    for g in range(GM_HEADS):
        cols = slice(g * GM_DIM, (g + 1) * GM_DIM)
        v = jax.nn.gelu(gv_ref[:, cols])
        mu = jnp.mean(v, axis=-1, keepdims=True)
        var = jnp.mean(jnp.square(v - mu), axis=-1, keepdims=True)
        vn = (v - mu) * lax.rsqrt(var + EPS) * ng_ref[:, cols] + nb_ref[:, cols]
        gvo_ref[:, cols] = vn
        wm = jnp.where(keep, w_ref[g], 0.0).astype(BF16)
        mixed = jnp.dot(wm, vn.astype(BF16), preferred_element_type=F32) + b_ref[:, g:g + 1]
        o_ref[:, cols] = jax.nn.gelu(gu_ref[:, cols]) * mixed


def chunk_mlp_call(z, wmix, bmix, ng, nb, l, *, n_ptiles, dec_len, col0):
    T = z.shape[0]
    GW = GM_HEADS * GM_DIM
    cb = col0 // GW
    row = pl.BlockSpec((CHUNK, GW), lambda i: (i, 0))
    sel = lambda i: jnp.where(i < n_ptiles, 0, 1)
    return pl.pallas_call(
        functools.partial(_gm_kernel, n_ptiles=n_ptiles, dec_len=dec_len),
        out_shape=[jax.ShapeDtypeStruct((T, GW), F32)] * 2,
        grid=(T // CHUNK,),
        in_specs=[
            pl.BlockSpec((CHUNK, GW), lambda i: (i, cb)),
            pl.BlockSpec((CHUNK, GW), lambda i: (i, cb + 1)),
            pl.BlockSpec((None, GM_HEADS, CHUNK, CHUNK), lambda i: (sel(i), 0, 0, 0)),
            pl.BlockSpec((None, CHUNK, GM_HEADS), lambda i: (sel(i), 0, 0)),
            pl.BlockSpec((None, 1, GW), lambda i: (l, 0, 0)),
            pl.BlockSpec((None, 1, GW), lambda i: (l, 0, 0)),
        ],
        out_specs=[row, row],
        compiler_params=_cparams(("arbitrary",)), name="chunk_mlp",
    )(z, z, wmix, bmix, ng, nb)


def _ffn_kernel(te_ref, xi_ref, act_ref, x_ref, wg_ref, wu_ref, wd_ref, o_ref):
    i, f = pl.program_id(0), pl.program_id(1)

    @pl.when(f == 0)
    def _():
        o_ref[...] = jnp.zeros(o_ref.shape, F32)

    @pl.when(act_ref[i] > 0)
    def _():
        x = x_ref[...]
        g = jnp.dot(x, wg_ref[...].astype(BF16), preferred_element_type=F32)
        u = jnp.dot(x, wu_ref[...].astype(BF16), preferred_element_type=F32)
        a = (g * jax.nn.sigmoid(g) * u).astype(BF16)
        o_ref[...] += jnp.dot(a, wd_ref[...].astype(BF16), preferred_element_type=F32)


def ffn_call(x, w_gate, w_up, w_down, wl, tile_expert, tile_x, tile_active, *, tm, tf=256):
    P, D = x.shape
    F = w_gate.shape[-1]
    tf = min(tf, F)
    nf = F // tf
    n_tiles = P // tm

    def fidx(i, f, act):
        return jnp.where(act[i] > 0, f, nf - 1)

    grid_spec = pltpu.PrefetchScalarGridSpec(
        num_scalar_prefetch=3,
        grid=(n_tiles, nf),
        in_specs=[
            pl.BlockSpec((tm, D), lambda i, f, te, xi, act: (xi[i], 0)),
            pl.BlockSpec((None, None, D, tf), lambda i, f, te, xi, act: (wl, te[i], 0, fidx(i, f, act))),
            pl.BlockSpec((None, None, D, tf), lambda i, f, te, xi, act: (wl, te[i], 0, fidx(i, f, act))),
            pl.BlockSpec((None, None, tf, D), lambda i, f, te, xi, act: (wl, te[i], fidx(i, f, act), 0)),
        ],
        out_specs=pl.BlockSpec((tm, D), lambda i, f, te, xi, act: (i, 0)),
    )
    return pl.pallas_call(
        _ffn_kernel,
        out_shape=jax.ShapeDtypeStruct((P, D), F32),
        grid_spec=grid_spec,
        compiler_params=_cparams(("arbitrary", "arbitrary")), name="ffn",
    )(tile_expert, tile_x, tile_active, x, w_gate, w_up, w_down)


def _split3(x):
    hi = x.astype(BF16)
    r1 = x - hi.astype(F32)
    mid = r1.astype(BF16)
    lo = (r1 - mid.astype(F32)).astype(BF16)
    return hi, mid, lo


def _router_kernel(h_ref, wr_ref, tri_ref, idx_ref, gate_ref, cnt_ref, carry_s, *, tm):
    i = pl.program_id(0)

    @pl.when(i == 0)
    def _():
        carry_s[...] = jnp.zeros(carry_s.shape, F32)

    hs = _split3(h_ref[...])
    ws = _split3(wr_ref[...])
    logits = jnp.zeros((N_EXPERTS, tm), F32)
    for a in range(3):
        for b in range(3 - a):
            logits += lax.dot_general(ws[a], hs[b], NT_DIMS, preferred_element_type=F32)
    eid = lax.broadcasted_iota(I32, (N_EXPERTS, tm), 0).astype(F32)
    none = float(N_EXPERTS)
    m1 = jnp.max(logits, axis=0, keepdims=True)
    e1 = jnp.min(jnp.where(logits == m1, eid, none), axis=0, keepdims=True)
    is1 = eid == e1
    rest = jnp.where(is1, -jnp.inf, logits)
    m2 = jnp.max(rest, axis=0, keepdims=True)
    e2 = jnp.min(jnp.where(rest == m2, eid, none), axis=0, keepdims=True)
    is2 = eid == e2
    ex = jnp.exp(m2 - m1)
    g1 = 1.0 / (1.0 + ex)
    g2 = ex / (1.0 + ex)
    sel = jnp.where(jnp.logical_or(is1, is2), 1.0, 0.0)
    before = jnp.dot(sel.astype(BF16), tri_ref[...], preferred_element_type=F32) + carry_s[...]
    r1 = jnp.sum(jnp.where(is1, before, 0.0), axis=0, keepdims=True)
    r2 = jnp.sum(jnp.where(is2, before, 0.0), axis=0, keepdims=True)
    carry_s[...] = carry_s[...] + jnp.sum(sel, axis=1, keepdims=True)
    idx_ref[0:1, :] = e1.astype(I32)
    idx_ref[1:2, :] = e2.astype(I32)
    idx_ref[2:3, :] = r1.astype(I32)
    idx_ref[3:4, :] = r2.astype(I32)
    idx_ref[4:8, :] = jnp.zeros((4, tm), I32)
    gate_ref[0:1, :] = g1
    gate_ref[1:2, :] = g2
    gate_ref[2:8, :] = jnp.zeros((6, tm), F32)
    cnt_ref[...] = jnp.broadcast_to(carry_s[...], cnt_ref.shape).astype(I32)


def router_call(h, w_router, wl, *, tm=512):
    T, D = h.shape
    tm = _fit_tile(T, tm)
    n = T // tm
    wr_t = jnp.swapaxes(w_router, 1, 2)
    tri = (jnp.arange(tm)[:, None] < jnp.arange(tm)[None, :]).astype(BF16)
    idx, gate, cnt = pl.pallas_call(
        functools.partial(_router_kernel, tm=tm),
        out_shape=[jax.ShapeDtypeStruct((n, 8, tm), I32),
                   jax.ShapeDtypeStruct((n, 8, tm), F32),
                   jax.ShapeDtypeStruct((N_EXPERTS, 128), I32)],
        grid=(n,),
        in_specs=[
            pl.BlockSpec((tm, D), lambda i: (i, 0)),
            pl.BlockSpec((None, N_EXPERTS, D), lambda i: (wl, 0, 0)),
            pl.BlockSpec((tm, tm), lambda i: (0, 0)),
        ],
        out_specs=[
            pl.BlockSpec((None, 8, tm), lambda i: (i, 0, 0)),
            pl.BlockSpec((None, 8, tm), lambda i: (i, 0, 0)),
            pl.BlockSpec((N_EXPERTS, 128), lambda i: (0, 0)),
        ],
        scratch_shapes=[pltpu.VMEM((N_EXPERTS, 1), F32)],
        compiler_params=_cparams(("arbitrary",)), name="router",
    )(h, wr_t, tri)
    rows = lambda a, k: a[:, k, :].reshape(T)
    return (rows(idx, 0), rows(idx, 1), rows(idx, 2), rows(idx, 3),
            rows(gate, 0), rows(gate, 1), cnt[:, 0])


def _start_row_copies(src_hbm, idx_ref, base, dst, sem, n):
    def body(r, carry):
        pltpu.make_async_copy(src_hbm.at[pl.ds(idx_ref[base + r], 1), :],
                              dst.at[pl.ds(r, 1), :], sem).start()
        return carry
    lax.fori_loop(0, n, body, 0)


def _wait_row_copies(src_hbm, dst, sem, n):
    pltpu.make_async_copy(src_hbm.at[pl.ds(0, n), :], dst, sem).wait()


def _gather_kernel(src_ref, nact_ref, h_hbm, o_ref, buf, sem, *, tg):
    i = pl.program_id(0)

    @pl.when(i < nact_ref[0])
    def _():
        _start_row_copies(h_hbm, src_ref, i * tg, buf, sem.at[0], tg)
        _wait_row_copies(h_hbm, buf, sem.at[0], tg)
        o_ref[...] = buf[...].astype(o_ref.dtype)

    @pl.when(i >= nact_ref[0])
    def _():
        o_ref[...] = jnp.zeros(o_ref.shape, o_ref.dtype)


def gather_call(h, src_token, n_active_blocks, *, n_slots, tg=256):
    T, D = h.shape
    grid_spec = pltpu.PrefetchScalarGridSpec(
        num_scalar_prefetch=2,
        grid=(n_slots // tg,),
        in_specs=[pl.BlockSpec(memory_space=pl.ANY)],
        out_specs=pl.BlockSpec((tg, D), lambda i, src, nact: (i, 0)),
        scratch_shapes=[pltpu.VMEM((tg, D), F32), pltpu.SemaphoreType.DMA((1,))],
    )
    return pl.pallas_call(
        functools.partial(_gather_kernel, tg=tg),
        out_shape=jax.ShapeDtypeStruct((n_slots, D), BF16),
        grid_spec=grid_spec,
        compiler_params=_cparams(("arbitrary",)), name="gather",
    )(src_token, n_active_blocks, h)


def _combine_kernel(s1_ref, s2_ref, y_hbm, g1_ref, g2_ref, x_ref, gpost_ref, gtp, gts, o_ref,
                    buf, sem, *, tc, n_ptiles):
    i = pl.program_id(0)
    _start_row_copies(y_hbm, s1_ref, i * tc, buf.at[0], sem.at[0], tc)
    _start_row_copies(y_hbm, s2_ref, i * tc, buf.at[1], sem.at[1], tc)
    _wait_row_copies(y_hbm, buf.at[0], sem.at[0], tc)
    _wait_row_copies(y_hbm, buf.at[1], sem.at[1], tc)
    f = g1_ref[...] * buf[0] + g2_ref[...] * buf[1]
    o_ref[...] = x_ref[...] + _mod_value(i, n_ptiles, gtp, gts) * _rms(f, gpost_ref[...])


def combine_call(geom, y, slot1, slot2, g1, g2, x, g_post, modp, mods, l, gt_col):
    T, D = x.shape
    tc = geom.tm
    row = pl.BlockSpec((tc, D), lambda i, *_: (i, 0))
    col = pl.BlockSpec((tc, 1), lambda i, *_: (i, 0))
    grid_spec = pltpu.PrefetchScalarGridSpec(
        num_scalar_prefetch=2,
        grid=(geom.n_tiles,),
        in_specs=[pl.BlockSpec(memory_space=pl.ANY), col, col, row,
                  pl.BlockSpec((None, 1, D), lambda i, *_: (l, 0, 0))]
                 + _mod_specs(geom, D, l, gt_col),
        out_specs=row,
        scratch_shapes=[pltpu.VMEM((2, tc, D), F32), pltpu.SemaphoreType.DMA((2,))],
    )
    return pl.pallas_call(
        functools.partial(_combine_kernel, tc=tc, n_ptiles=geom.n_ptiles),
        out_shape=jax.ShapeDtypeStruct((T, D), F32),
        grid_spec=grid_spec,
        compiler_params=_cparams(("arbitrary",)), name="combine",
    )(slot1, slot2, y, g1.reshape(T, 1), g2.reshape(T, 1), x, g_post, modp, mods)


def moe_dispatch_plan(e1, e2, r1, r2, counts, *, tm, n_tiles_max):
    T = e1.shape[0]
    tiles_e = (counts + tm - 1) // tm
    tile_end = jnp.cumsum(tiles_e)
    tile_off = tile_end - tiles_e
    n_active = tile_end[-1]
    slot1 = tile_off[e1] * tm + r1
    slot2 = tile_off[e2] * tm + r2
    tok = jnp.arange(T, dtype=I32)
    src = jnp.zeros((n_tiles_max * tm,), I32).at[slot1].set(tok).at[slot2].set(tok)
    t = jnp.arange(n_tiles_max, dtype=I32)
    tile_x = jnp.minimum(t, n_active - 1).astype(I32)
    tile_expert = jnp.minimum(jnp.searchsorted(tile_end, tile_x, side="right"), N_EXPERTS - 1).astype(I32)
    tile_active = (t < n_active).astype(I32)
    return slot1.astype(I32), slot2.astype(I32), src, tile_expert, tile_x, tile_active, n_active.astype(I32)


def kernel(x_prompt, x_sample, c_prompt, c_sample, cache_k, cache_v, page_table, g_pre_mix, g_post_mix, g_pre_ffn, g_post_ffn, w_ada, b_ada, w_in, w_out, lam_q1, lam_k1, lam_q2, lam_k2, g_subln, gm_ws, gm_b, gm_norm_g, gm_norm_b, w_d_gate, w_d_up, w_d_down, w_router, w_e_gate, w_e_up, w_e_down):
    B, S, D = x_prompt.shape
    Bd, Td, _ = x_sample.shape
    L = w_in.shape[0]
    NP, NS = B * S, Bd * Td
    T = NP + NS
    H = DA_HEADS
    QKW = 2 * DA_HEAD_DIM * H
    VW = DA_VDIM * H
    GW = GM_HEADS * GM_DIM
    assert S % CHUNK == 0 and NS % CHUNK == 0 and CHUNK % Td == 0

    tm_norm = min(256, NS)
    geom = RowGeom(B, S, Bd, Td, tm_norm)
    tm_ffn = min(1024, NS)
    assert T % tm_ffn == 0

    x = jnp.concatenate([x_prompt.reshape(NP, D), x_sample.reshape(NS, D)], axis=0)
    n_c = B + Bd
    c_all = jnp.concatenate([c_prompt, c_sample, jnp.zeros((-n_c % 8, D), F32)], axis=0)
    mod = ada_call(c_all, w_ada, b_ada)
    modp = mod[:, :B].reshape(L, B, 1, 6 * D)
    mods = jnp.repeat(mod[:, B:n_c], Td, axis=1)
    SH_M, SC_M, GT_M, SH_F, SC_F, GT_F = range(6)

    g3 = lambda a: a.reshape(L, 1, a.shape[-1])
    g_pre_mix3, g_post_mix3, g_pre_ffn3, g_post_ffn3 = map(g3, (g_pre_mix, g_post_mix, g_pre_ffn, g_post_ffn))
    g_subln3 = g3(g_subln)
    ng3 = gm_norm_g.reshape(L, 1, GW)
    nb3 = gm_norm_b.reshape(L, 1, GW)
    lamv = jnp.stack([lam_q1, lam_k1, lam_q2, lam_k2], axis=1)

    n_dense_tiles = T // tm_ffn
    dense_te = jnp.zeros((n_dense_tiles,), I32)
    dense_xi = jnp.arange(n_dense_tiles, dtype=I32)
    dense_act = jnp.ones((n_dense_tiles,), I32)

    (h,) = norm_call(geom, x, modp, mods, g_pre=g_pre_mix3, l_pre=0, sc_col=SC_M, sh_col=SH_M)

    k_p, v_p, k_s, v_s, gv_s = [], [], [], [], []
    for l in range(L):
        lam_init = 0.8 - 0.6 * math.exp(-0.3 * l)
        z = mm_call([h], w_in, l)
        k_p.append(z[:NP, QKW:2 * QKW].reshape(B, S, H, 2, DA_HEAD_DIM))
        v_p.append(z[:NP, 2 * QKW:2 * QKW + VW].reshape(B, S, H, DA_VDIM))
        k_s.append(z[NP:, QKW:2 * QKW].reshape(Bd, Td, H, 2, DA_HEAD_DIM))
        v_s.append(z[NP:, 2 * QKW:2 * QKW + VW].reshape(Bd, Td, H, DA_VDIM))

        o_att_p = attn_prompt_call(z, lamv, g_subln3, l, n_batch=B, seq=S, lam_init=lam_init)
        o_att_s = attn_sample_call(z, cache_k, cache_v, page_table, lamv, g_subln3, l,
                                   row0=NP, dec_len=Td, lam_init=lam_init)
        o_att = jnp.concatenate([o_att_p, o_att_s], axis=0)

        reps = CHUNK // Td
        wmix = jnp.stack([gm_ws[l], jnp.tile(gm_ws[l][:, :Td, :Td], (1, reps, reps))])
        bmix = jnp.stack([gm_b[l].T, jnp.tile(gm_b[l][:, :Td], (1, reps)).T])
        o_gm, gvn = chunk_mlp_call(z, wmix, bmix, ng3, nb3, l, n_ptiles=NP // CHUNK, dec_len=Td,
                                   col0=2 * QKW + VW)
        gv_s.append(gvn[NP:].reshape(Bd, Td, GM_HEADS, GM_DIM))

        m = mm_call([o_att, o_gm], w_out, l)
        is_moe = l % 2 == 1
        x, h2 = norm_call(geom, x, modp, mods, m=m, g_post=g_post_mix3, l_post=l, gt_col=GT_M,
                          g_pre=g_pre_ffn3, l_pre=l, sc_col=SC_F, sh_col=SH_F,
                          h_dtype=F32 if is_moe else BF16)
        wl = l // 2
        if not is_moe:
            f = ffn_call(h2, w_d_gate[:, None], w_d_up[:, None], w_d_down[:, None], wl,
                         dense_te, dense_xi, dense_act, tm=tm_ffn)
            if l + 1 < L:
                x, h = norm_call(geom, x, modp, mods, m=f, g_post=g_post_ffn3, l_post=l, gt_col=GT_F,
                                 g_pre=g_pre_mix3, l_pre=l + 1, sc_col=SC_M, sh_col=SH_M)
            else:
                (x,) = norm_call(geom, x, modp, mods, m=f, g_post=g_post_ffn3, l_post=l, gt_col=GT_F)
        else:
            e1, e2, r1, r2, g1, g2, counts = router_call(h2, w_router, wl)
            n_tiles_max = (2 * T + N_EXPERTS * (tm_ffn - 1)) // tm_ffn
            slot1, slot2, src, te, xi, act, n_active = moe_dispatch_plan(
                e1, e2, r1, r2, counts, tm=tm_ffn, n_tiles_max=n_tiles_max)
            tg = min(256, tm_ffn)
            xs = gather_call(h2, src, (n_active * (tm_ffn // tg)).reshape(1),
                             n_slots=n_tiles_max * tm_ffn, tg=tg)
            y = ffn_call(xs, w_e_gate, w_e_up, w_e_down, wl, te, xi, act, tm=tm_ffn)
            x = combine_call(geom, y, slot1, slot2, g1, g2, x, g_post_ffn3, modp, mods, l, GT_F)
            if l + 1 < L:
                (h,) = norm_call(geom, x, modp, mods, g_pre=g_pre_mix3, l_pre=l + 1, sc_col=SC_M, sh_col=SH_M)

    y_prompt = x[:NP].reshape(B, S, D)
    y_sample = x[NP:].reshape(Bd, Td, D)
    return (y_prompt, y_sample, jnp.stack(k_p), jnp.stack(v_p), jnp.stack(k_s), jnp.stack(v_s),
            jnp.stack(gv_s))
```

```python
import functools
import math

import jax
import jax.numpy as jnp
from jax import lax
from jax.experimental import pallas as pl
from jax.experimental.pallas import tpu as pltpu

F32 = jnp.float32
BF16 = jnp.bfloat16
I32 = jnp.int32

EPS = 1e-6
DA_HEADS = 8
DA_HEAD_DIM = 64
DA_VDIM = 128
GM_HEADS = 8
GM_DIM = 128
CHUNK = 128
PAGE_SIZE = 128
N_EXPERTS = 8
NEG = -1e30
LOG2E = 1.4426950408889634

V7X_VMEM_BYTES = 64 * 1024 * 1024
VMEM_LIMIT = V7X_VMEM_BYTES - 8 * 1024 * 1024

NT_DIMS = (((1,), (1,)), ((), ()))


def _cparams(sem):
    return pltpu.CompilerParams(dimension_semantics=sem, vmem_limit_bytes=VMEM_LIMIT)


def _fit_tile(n, t):
    while n % t:
        t //= 2
    return t


def _rms(x, g):
    return x * lax.rsqrt(jnp.mean(x * x, axis=-1, keepdims=True) + EPS) * g


def _ada_kernel(c_ref, w_ref, b_ref, o_ref):
    c = c_ref[...]
    a = (c * jax.nn.sigmoid(c)).astype(BF16)
    o_ref[...] = jnp.dot(a, w_ref[...].astype(BF16), preferred_element_type=F32) + b_ref[...]


def ada_call(c_all, w_ada, b_ada):
    R, D = c_all.shape
    L, _, N = w_ada.shape
    tn = min(1024, N)
    return pl.pallas_call(
        _ada_kernel,
        out_shape=jax.ShapeDtypeStruct((L, R, N), F32),
        grid=(L, N // tn),
        in_specs=[
            pl.BlockSpec((R, D), lambda l, j: (0, 0)),
            pl.BlockSpec((None, D, tn), lambda l, j: (l, 0, j)),
            pl.BlockSpec((None, 1, tn), lambda l, j: (l, 0, j)),
        ],
        out_specs=pl.BlockSpec((None, R, tn), lambda l, j: (l, 0, j)),
        compiler_params=_cparams(("arbitrary", "arbitrary")),
        name="ada",
    )(c_all, w_ada, b_ada.reshape(L, 1, N))


class RowGeom:
    def __init__(self, n_batch, seq, n_dec, dec_len, tm):
        assert seq % tm == 0 and (n_dec * dec_len) % tm == 0
        self.tm = tm
        self.n_batch = n_batch
        self.tiles_per_batch = seq // tm
        self.n_ptiles = n_batch * self.tiles_per_batch
        self.n_tiles = self.n_ptiles + (n_dec * dec_len) // tm


def _mod_specs(geom, D, l, col):
    tpb, npt, nb = geom.tiles_per_batch, geom.n_ptiles, geom.n_batch
    sp = pl.BlockSpec((None, None, 1, D),
                      lambda i, *_: (l, jnp.minimum(i // tpb, nb - 1), 0, col))
    ss = pl.BlockSpec((None, geom.tm, D),
                      lambda i, *_: (l, jnp.maximum(i - npt, 0), col))
    return [sp, ss]


def _mod_value(i, n_ptiles, p_ref, s_ref):
    return jnp.where(i < n_ptiles, p_ref[...], s_ref[...])


def _norm_kernel(*refs, n_ptiles, post, pre):
    it = iter(refs)
    x_ref = next(it)
    if post:
        m_ref, gpost_ref, gtp, gts = next(it), next(it), next(it), next(it)
    if pre:
        gpre_ref, scp, scs, shp, shs = next(it), next(it), next(it), next(it), next(it)
    if post:
        xo_ref = next(it)
    if pre:
        h_ref = next(it)
    i = pl.program_id(0)
    x = x_ref[...]
    if post:
        x = x + _mod_value(i, n_ptiles, gtp, gts) * _rms(m_ref[...], gpost_ref[...])
        xo_ref[...] = x
    if pre:
        h = _rms(x, gpre_ref[...]) * (1.0 + _mod_value(i, n_ptiles, scp, scs)) \
            + _mod_value(i, n_ptiles, shp, shs)
        h_ref[...] = h.astype(h_ref.dtype)


def norm_call(geom, x, modp, mods, *, m=None, g_post=None, l_post=None, gt_col=None,
              g_pre=None, l_pre=None, sc_col=None, sh_col=None, h_dtype=BF16):
    T, D = x.shape
    tm = geom.tm
    post, pre = m is not None, g_pre is not None
    row = pl.BlockSpec((tm, D), lambda i: (i, 0))
    args, specs = [x], [row]
    if post:
        args += [m, g_post, modp, mods]
        specs += [row, pl.BlockSpec((None, 1, D), lambda i: (l_post, 0, 0))]
        specs += _mod_specs(geom, D, l_post, gt_col)
    if pre:
        args += [g_pre, modp, mods, modp, mods]
        specs += [pl.BlockSpec((None, 1, D), lambda i: (l_pre, 0, 0))]
        specs += _mod_specs(geom, D, l_pre, sc_col) + _mod_specs(geom, D, l_pre, sh_col)
    out_shape, out_specs = [], []
    if post:
        out_shape.append(jax.ShapeDtypeStruct((T, D), F32))
        out_specs.append(row)
    if pre:
        out_shape.append(jax.ShapeDtypeStruct((T, D), h_dtype))
        out_specs.append(row)
    outs = pl.pallas_call(
        functools.partial(_norm_kernel, n_ptiles=geom.n_ptiles, post=post, pre=pre),
        out_shape=out_shape, grid=(geom.n_tiles,), in_specs=specs, out_specs=out_specs,
        compiler_params=_cparams(("arbitrary",)), name="norm",
    )(*args)
    return outs


def _mm_kernel(*refs, n_a):
    o_ref = refs[2 * n_a]
    acc = None
    for a_ref, w_ref in zip(refs[:n_a], refs[n_a:2 * n_a]):
        d = jnp.dot(a_ref[...].astype(BF16), w_ref[...].astype(BF16), preferred_element_type=F32)
        acc = d if acc is None else acc + d
    o_ref[...] = acc


def mm_call(a_list, w, l, *, tm=1024, tn=512):
    T = a_list[0].shape[0]
    N = w.shape[2]
    tm, tn = _fit_tile(T, tm), _fit_tile(N, tn)
    specs = [pl.BlockSpec((tm, a.shape[1]), lambda i, j: (i, 0)) for a in a_list]
    for k, a in enumerate(a_list):
        specs.append(pl.BlockSpec((None, a.shape[1], tn), lambda i, j, k=k: (l, k, j)))
    return pl.pallas_call(
        functools.partial(_mm_kernel, n_a=len(a_list)),
        out_shape=jax.ShapeDtypeStruct((T, N), F32),
        grid=(T // tm, N // tn), in_specs=specs,
        out_specs=pl.BlockSpec((tm, tn), lambda i, j: (i, j)),
        compiler_params=_cparams(("arbitrary", "arbitrary")), name="matmul",
    )(*a_list, *([w] * len(a_list)))


def _lam_value(lamv_ref, lam_init):
    v = lamv_ref[...]
    s1 = jnp.sum(v[0:1] * v[1:2], axis=-1, keepdims=True)
    s2 = jnp.sum(v[2:3] * v[3:4], axis=-1, keepdims=True)
    return jnp.exp(s1) - jnp.exp(s2) + lam_init


def _alibi_slope(head):
    return lax.bitcast_convert_type((126 - head) << 23, F32)


def _attn_p_kernel(qi_tab, ki_tab, lamv_ref, q_ref, k_ref, v_ref, g_ref, o_ref,
                   q_s, m_s, l_s, acc_s, *, tq, lam_init):
    h = pl.program_id(1)
    p = pl.program_id(2)
    qi, ki = qi_tab[p], ki_tab[p]

    @pl.when(ki == 0)
    def _():
        q = q_ref[...] * (DA_HEAD_DIM ** -0.5 * LOG2E)
        lane = lax.broadcasted_iota(I32, q.shape, 1)
        q_s[0] = jnp.where(lane < DA_HEAD_DIM, q, 0.0).astype(BF16)
        q_s[1] = jnp.where(lane >= DA_HEAD_DIM, q, 0.0).astype(BF16)
        m_s[...] = jnp.full(m_s.shape, NEG, F32)
        l_s[...] = jnp.zeros(l_s.shape, F32)
        acc_s[...] = jnp.zeros(acc_s.shape, F32)

    def update(masked):
        k = k_ref[...].astype(BF16)
        vt = v_ref[...].T.astype(BF16)
        krow = lax.broadcasted_iota(I32, (tq, 128), 0)
        slope = _alibi_slope(jnp.zeros((tq, 128), I32) + h) * LOG2E
        bias = slope * ((ki - qi) * tq + krow).astype(F32)
        bias = jnp.concatenate([bias] * (tq // 128), axis=1)
        if masked:
            kr = lax.broadcasted_iota(I32, (tq, tq), 0)
            qc = lax.broadcasted_iota(I32, (tq, tq), 1)
            bias = jnp.where(kr <= qc, bias, NEG)
        for mi in range(2):
            s = lax.dot_general(k, q_s[mi], NT_DIMS, preferred_element_type=F32) + bias
            m_old = m_s[mi]
            m_new = jnp.maximum(m_old, jnp.max(s, axis=0, keepdims=True))
            alpha = jnp.exp2(m_old - m_new)
            pm = jnp.exp2(s - m_new)
            l_s[mi] = alpha * l_s[mi] + jnp.sum(pm, axis=0, keepdims=True)
            acc_s[mi] = alpha * acc_s[mi] + jnp.dot(vt, pm.astype(BF16), preferred_element_type=F32)
            m_s[mi] = m_new

    @pl.when(ki < qi)
    def _():
        update(False)

    @pl.when(ki == qi)
    def _():
        update(True)
        lam = _lam_value(lamv_ref, lam_init)
        o = acc_s[0] / l_s[0] - lam * (acc_s[1] / l_s[1])
        ms = jnp.mean(o * o, axis=0, keepdims=True)
        on = o * lax.rsqrt(ms + EPS) * g_ref[...] * (1.0 - lam_init)
        o_ref[...] = on.T


def attn_prompt_call(z, lamv, g_subln_col, l, *, n_batch, seq, lam_init, tq=512):
    tq = min(tq, seq)
    nq = seq // tq
    pairs = [(qi, ki) for qi in range(nq) for ki in range(qi + 1)]
    qi_tab = jnp.asarray([p[0] for p in pairs], I32)
    ki_tab = jnp.asarray([p[1] for p in pairs], I32)
    H = DA_HEADS
    w = 2 * DA_HEAD_DIM
    grid_spec = pltpu.PrefetchScalarGridSpec(
        num_scalar_prefetch=2,
        grid=(n_batch, H, len(pairs)),
        in_specs=[
            pl.BlockSpec((None, 4, DA_HEAD_DIM), lambda b, h, p, qt, kt: (l, 0, 0)),
            pl.BlockSpec((tq, w), lambda b, h, p, qt, kt: (b * nq + qt[p], h)),
            pl.BlockSpec((tq, w), lambda b, h, p, qt, kt: (b * nq + kt[p], H + h)),
            pl.BlockSpec((tq, DA_VDIM), lambda b, h, p, qt, kt: (b * nq + kt[p], 2 * H + h)),
            pl.BlockSpec((None, DA_VDIM, 1), lambda b, h, p, qt, kt: (l, 0, 0)),
        ],
        out_specs=pl.BlockSpec((tq, DA_VDIM), lambda b, h, p, qt, kt: (b * nq + qt[p], h)),
        scratch_shapes=[
            pltpu.VMEM((2, tq, w), BF16),
            pltpu.VMEM((2, 1, tq), F32),
            pltpu.VMEM((2, 1, tq), F32),
            pltpu.VMEM((2, DA_VDIM, tq), F32),
        ],
    )
    return pl.pallas_call(
        functools.partial(_attn_p_kernel, tq=tq, lam_init=lam_init),
        out_shape=jax.ShapeDtypeStruct((n_batch * seq, H * DA_VDIM), F32),
        grid_spec=grid_spec,
        compiler_params=_cparams(("arbitrary", "arbitrary", "arbitrary")),
        name="attn_prompt",
    )(qi_tab, ki_tab, lamv, z, z, z, g_subln_col)


def _attn_s_kernel(pt_ref, lamv_ref, q_ref, kn_ref, vn_ref, g_ref, ckt_hbm, cv_hbm, o_ref,
                   kbuf, vbuf, sem, *, layer, n_dec, n_pages, dec_len, lam_init):
    H, W = DA_HEADS, 2 * DA_HEAD_DIM * DA_HEADS
    R = 2 * H * dec_len
    past = n_pages * PAGE_SIZE
    b = pl.program_id(0)
    slot = b % 2

    def start_pages(seq, s):
        for pg in range(n_pages):
            page = pt_ref[seq * n_pages + pg]
            pltpu.make_async_copy(ckt_hbm.at[layer, page], kbuf.at[s, pg], sem.at[s, 0]).start()
            pltpu.make_async_copy(cv_hbm.at[layer, page], vbuf.at[s, pg], sem.at[s, 1]).start()

    @pl.when(b == 0)
    def _():
        start_pages(0, 0)

    @pl.when(b + 1 < n_dec)
    def _():
        start_pages(b + 1, 1 - slot)

    pltpu.make_async_copy(ckt_hbm.at[layer, pl.ds(0, n_pages)], kbuf.at[slot], sem.at[slot, 0]).wait()
    pltpu.make_async_copy(cv_hbm.at[layer, pl.ds(0, n_pages)], vbuf.at[slot], sem.at[slot, 1]).wait()

    q = q_ref[...] * (DA_HEAD_DIM ** -0.5 * LOG2E)
    qt = jnp.broadcast_to(q[None], (2 * H, dec_len, W)).reshape(R, W)
    r = lax.broadcasted_iota(I32, (R, W), 0)
    c = lax.broadcasted_iota(I32, (R, W), 1)
    row_map, row_head = r // (H * dec_len), (r // dec_len) % H
    col_head, col_map = c // (2 * DA_HEAD_DIM), (c // DA_HEAD_DIM) % 2
    keep = jnp.logical_and(row_map == col_map, row_head == col_head)
    qx = jnp.where(keep, qt, 0.0).astype(BF16)
    lane = lax.broadcasted_iota(I32, (R, PAGE_SIZE), 1)
    rr = lax.broadcasted_iota(I32, (R, PAGE_SIZE), 0)
    slope = _alibi_slope((rr // dec_len) % H) * LOG2E
    tiles = []
    for pg in range(n_pages):
        kt = kbuf[slot, pg].astype(BF16)
        bias = slope * (lane + (pg * PAGE_SIZE - past)).astype(F32)
        tiles.append(jnp.dot(qx, kt, preferred_element_type=F32) + bias)
    zpad = jnp.zeros((PAGE_SIZE - dec_len, W), F32)
    kn = jnp.concatenate([kn_ref[...], zpad], axis=0).astype(BF16)
    s_new = lax.dot_general(qx, kn, NT_DIMS, preferred_element_type=F32) + slope * lane.astype(F32)
    tiles.append(jnp.where(lane <= rr % dec_len, s_new, NEG))

    m = tiles[0]
    for t in tiles[1:]:
        m = jnp.maximum(m, t)
    m = jnp.max(m, axis=-1, keepdims=True)
    ps = [jnp.exp2(t - m) for t in tiles]
    lsum = ps[0]
    for t in ps[1:]:
        lsum = lsum + t
    inv = 1.0 / jnp.sum(lsum, axis=-1, keepdims=True)

    lam = _lam_value(lamv_ref, lam_init)
    hq = H * dec_len
    for h in range(H):
        rows1 = slice(h * dec_len, (h + 1) * dec_len)
        rows2 = slice(hq + h * dec_len, hq + (h + 1) * dec_len)
        cols = slice(h * DA_VDIM, (h + 1) * DA_VDIM)
        ph = jnp.concatenate(
            [jnp.concatenate([t[rows1], t[rows2]], axis=0).astype(BF16) for t in ps], axis=1)
        vparts = [vbuf[slot, pg, pl.ds(h, PAGE_SIZE, stride=H), :].astype(BF16) for pg in range(n_pages)]
        vparts.append(jnp.concatenate([vn_ref[:, cols], zpad[:, :DA_VDIM]], axis=0).astype(BF16))
        acc = jnp.dot(ph, jnp.concatenate(vparts, axis=0), preferred_element_type=F32)
        o = acc[:dec_len] * inv[rows1] - lam * (acc[dec_len:] * inv[rows2])
        o_ref[:, cols] = _rms(o, g_ref[...]) * (1.0 - lam_init)


def attn_sample_call(z, cache_k, cache_v, page_table, lamv, g_subln, l, *, row0, dec_len, lam_init):
    n_dec, n_pages = page_table.shape
    H = DA_HEADS
    W = 2 * DA_HEAD_DIM * H
    n_layers, n_pool = cache_k.shape[:2]
    ckt = jnp.transpose(cache_k, (0, 1, 3, 4, 5, 2)).reshape(n_layers, n_pool, W, PAGE_SIZE)
    cv = cache_v.reshape(n_layers, n_pool, PAGE_SIZE * H, DA_VDIM)
    rb = row0 // dec_len
    grid_spec = pltpu.PrefetchScalarGridSpec(
        num_scalar_prefetch=1,
        grid=(n_dec,),
        in_specs=[
            pl.BlockSpec((None, 4, DA_HEAD_DIM), lambda b, pt: (l, 0, 0)),
            pl.BlockSpec((dec_len, W), lambda b, pt: (rb + b, 0)),
            pl.BlockSpec((dec_len, W), lambda b, pt: (rb + b, 1)),
            pl.BlockSpec((dec_len, W), lambda b, pt: (rb + b, 2)),
            pl.BlockSpec((None, 1, DA_VDIM), lambda b, pt: (l, 0, 0)),
            pl.BlockSpec(memory_space=pl.ANY),
            pl.BlockSpec(memory_space=pl.ANY),
        ],
        out_specs=pl.BlockSpec((dec_len, W), lambda b, pt: (b, 0)),
        scratch_shapes=[
            pltpu.VMEM((2, n_pages, W, PAGE_SIZE), F32),
            pltpu.VMEM((2, n_pages, PAGE_SIZE * H, DA_VDIM), F32),
            pltpu.SemaphoreType.DMA((2, 2)),
        ],
    )
    return pl.pallas_call(
        functools.partial(_attn_s_kernel, layer=l, n_dec=n_dec, n_pages=n_pages, dec_len=dec_len,
                          lam_init=lam_init),
        out_shape=jax.ShapeDtypeStruct((n_dec * dec_len, W), F32),
        grid_spec=grid_spec,
        compiler_params=_cparams(("arbitrary",)),
        name="attn_sample",
    )(page_table.reshape(-1), lamv, z, z, z, g_subln, ckt, cv)


def _gm_kernel(gu_ref, gv_ref, w_ref, b_ref, ng_ref, nb_ref, o_ref, gvo_ref, *, n_ptiles, dec_len):
    i = pl.program_id(0)
    r = lax.broadcasted_iota(I32, (CHUNK, CHUNK), 0)
    c = lax.broadcasted_iota(I32, (CHUNK, CHUNK), 1)
    same_seq = jnp.where(r // dec_len == c // dec_len, 1, 0) + jnp.where(i < n_ptiles, 1, 0)
    keep = jnp.logical_and(c <= r, same_seq > 0)
    for g in range(GM_HEADS):
        cols = slice(g * GM_DIM, (g + 1) * GM_DIM)
        v = jax.nn.gelu(gv_ref[:, cols])
        mu = jnp.mean(v, axis=-1, keepdims=True)
        var = jnp.mean(jnp.square(v - mu), axis=-1, keepdims=True)
        vn = (v - mu) * lax.rsqrt(var + EPS) * ng_ref[:, cols] + nb_ref[:, cols]
        gvo_ref[:, cols] = vn
        wm = jnp.where(keep, w_ref[g], 0.0).astype(BF16)
        mixed = jnp.dot(wm, vn.astype(BF16), preferred_element_type=F32) + b_ref[:, g:g + 1]
        o_ref[:, cols] = jax.nn.gelu(gu_ref[:, cols]) * mixed


def chunk_mlp_call(z, wmix, bmix, ng, nb, l, *, n_ptiles, dec_len, col0):
    T = z.shape[0]
    GW = GM_HEADS * GM_DIM
    cb = col0 // GW
    row = pl.BlockSpec((CHUNK, GW), lambda i: (i, 0))
    sel = lambda i: jnp.where(i < n_ptiles, 0, 1)
    return pl.pallas_call(
        functools.partial(_gm_kernel, n_ptiles=n_ptiles, dec_len=dec_len),
        out_shape=[jax.ShapeDtypeStruct((T, GW), F32)] * 2,
        grid=(T // CHUNK,),
        in_specs=[
            pl.BlockSpec((CHUNK, GW), lambda i: (i, cb)),
            pl.BlockSpec((CHUNK, GW), lambda i: (i, cb + 1)),
            pl.BlockSpec((None, GM_HEADS, CHUNK, CHUNK), lambda i: (sel(i), 0, 0, 0)),
            pl.BlockSpec((None, CHUNK, GM_HEADS), lambda i: (sel(i), 0, 0)),
            pl.BlockSpec((None, 1, GW), lambda i: (l, 0, 0)),
            pl.BlockSpec((None, 1, GW), lambda i: (l, 0, 0)),
        ],
        out_specs=[row, row],
        compiler_params=_cparams(("arbitrary",)), name="chunk_mlp",
    )(z, z, wmix, bmix, ng, nb)


def _ffn_kernel(te_ref, xi_ref, act_ref, x_ref, wg_ref, wu_ref, wd_ref, o_ref):
    i, f = pl.program_id(0), pl.program_id(1)

    @pl.when(f == 0)
    def _():
        o_ref[...] = jnp.zeros(o_ref.shape, F32)

    @pl.when(act_ref[i] > 0)
    def _():
        x = x_ref[...]
        g = jnp.dot(x, wg_ref[...].astype(BF16), preferred_element_type=F32)
        u = jnp.dot(x, wu_ref[...].astype(BF16), preferred_element_type=F32)
        a = (g * jax.nn.sigmoid(g) * u).astype(BF16)
        o_ref[...] += jnp.dot(a, wd_ref[...].astype(BF16), preferred_element_type=F32)


def ffn_call(x, w_gate, w_up, w_down, wl, tile_expert, tile_x, tile_active, *, tm, tf=256):
    P, D = x.shape
    F = w_gate.shape[-1]
    tf = min(tf, F)
    nf = F // tf
    n_tiles = P // tm

    def fidx(i, f, act):
        return jnp.where(act[i] > 0, f, nf - 1)

    grid_spec = pltpu.PrefetchScalarGridSpec(
        num_scalar_prefetch=3,
        grid=(n_tiles, nf),
        in_specs=[
            pl.BlockSpec((tm, D), lambda i, f, te, xi, act: (xi[i], 0)),
            pl.BlockSpec((None, None, D, tf), lambda i, f, te, xi, act: (wl, te[i], 0, fidx(i, f, act))),
            pl.BlockSpec((None, None, D, tf), lambda i, f, te, xi, act: (wl, te[i], 0, fidx(i, f, act))),
            pl.BlockSpec((None, None, tf, D), lambda i, f, te, xi, act: (wl, te[i], fidx(i, f, act), 0)),
        ],
        out_specs=pl.BlockSpec((tm, D), lambda i, f, te, xi, act: (i, 0)),
    )
    return pl.pallas_call(
        _ffn_kernel,
        out_shape=jax.ShapeDtypeStruct((P, D), F32),
        grid_spec=grid_spec,
        compiler_params=_cparams(("arbitrary", "arbitrary")), name="ffn",
    )(tile_expert, tile_x, tile_active, x, w_gate, w_up, w_down)


def _split3(x):
    hi = x.astype(BF16)
    r1 = x - hi.astype(F32)
    mid = r1.astype(BF16)
    lo = (r1 - mid.astype(F32)).astype(BF16)
    return hi, mid, lo


def _router_kernel(h_ref, wr_ref, tri_ref, idx_ref, gate_ref, cnt_ref, carry_s, *, tm):
    i = pl.program_id(0)

    @pl.when(i == 0)
    def _():
        carry_s[...] = jnp.zeros(carry_s.shape, F32)

    hs = _split3(h_ref[...])
    ws = _split3(wr_ref[...])
    logits = jnp.zeros((N_EXPERTS, tm), F32)
    for a in range(3):
        for b in range(3 - a):
            logits += lax.dot_general(ws[a], hs[b], NT_DIMS, preferred_element_type=F32)
    eid = lax.broadcasted_iota(I32, (N_EXPERTS, tm), 0).astype(F32)
    none = float(N_EXPERTS)
    m1 = jnp.max(logits, axis=0, keepdims=True)
    e1 = jnp.min(jnp.where(logits == m1, eid, none), axis=0, keepdims=True)
    is1 = eid == e1
    rest = jnp.where(is1, -jnp.inf, logits)
    m2 = jnp.max(rest, axis=0, keepdims=True)
    e2 = jnp.min(jnp.where(rest == m2, eid, none), axis=0, keepdims=True)
    is2 = eid == e2
    ex = jnp.exp(m2 - m1)
    g1 = 1.0 / (1.0 + ex)
    g2 = ex / (1.0 + ex)
    sel = jnp.where(jnp.logical_or(is1, is2), 1.0, 0.0)
    before = jnp.dot(sel.astype(BF16), tri_ref[...], preferred_element_type=F32) + carry_s[...]
    r1 = jnp.sum(jnp.where(is1, before, 0.0), axis=0, keepdims=True)
    r2 = jnp.sum(jnp.where(is2, before, 0.0), axis=0, keepdims=True)
    carry_s[...] = carry_s[...] + jnp.sum(sel, axis=1, keepdims=True)
    idx_ref[0:1, :] = e1.astype(I32)
    idx_ref[1:2, :] = e2.astype(I32)
    idx_ref[2:3, :] = r1.astype(I32)
    idx_ref[3:4, :] = r2.astype(I32)
    idx_ref[4:8, :] = jnp.zeros((4, tm), I32)
    gate_ref[0:1, :] = g1
    gate_ref[1:2, :] = g2
    gate_ref[2:8, :] = jnp.zeros((6, tm), F32)
    cnt_ref[...] = jnp.broadcast_to(carry_s[...], cnt_ref.shape).astype(I32)


def router_call(h, w_router, wl, *, tm=512):
    T, D = h.shape
    tm = _fit_tile(T, tm)
    n = T // tm
    wr_t = jnp.swapaxes(w_router, 1, 2)
    tri = (jnp.arange(tm)[:, None] < jnp.arange(tm)[None, :]).astype(BF16)
    idx, gate, cnt = pl.pallas_call(
        functools.partial(_router_kernel, tm=tm),
        out_shape=[jax.ShapeDtypeStruct((n, 8, tm), I32),
                   jax.ShapeDtypeStruct((n, 8, tm), F32),
                   jax.ShapeDtypeStruct((N_EXPERTS, 128), I32)],
        grid=(n,),
        in_specs=[
            pl.BlockSpec((tm, D), lambda i: (i, 0)),
            pl.BlockSpec((None, N_EXPERTS, D), lambda i: (wl, 0, 0)),
            pl.BlockSpec((tm, tm), lambda i: (0, 0)),
        ],
        out_specs=[
            pl.BlockSpec((None, 8, tm), lambda i: (i, 0, 0)),
            pl.BlockSpec((None, 8, tm), lambda i: (i, 0, 0)),
            pl.BlockSpec((N_EXPERTS, 128), lambda i: (0, 0)),
        ],
        scratch_shapes=[pltpu.VMEM((N_EXPERTS, 1), F32)],
        compiler_params=_cparams(("arbitrary",)), name="router",
    )(h, wr_t, tri)
    rows = lambda a, k: a[:, k, :].reshape(T)
    return (rows(idx, 0), rows(idx, 1), rows(idx, 2), rows(idx, 3),
            rows(gate, 0), rows(gate, 1), cnt[:, 0])


def _start_row_copies(src_hbm, idx_ref, base, dst, sem, n):
    def body(r, carry):
        pltpu.make_async_copy(src_hbm.at[pl.ds(idx_ref[base + r], 1), :],
                              dst.at[pl.ds(r, 1), :], sem).start()
        return carry
    lax.fori_loop(0, n, body, 0)


def _wait_row_copies(src_hbm, dst, sem, n):
    pltpu.make_async_copy(src_hbm.at[pl.ds(0, n), :], dst, sem).wait()


def _gather_kernel(src_ref, nact_ref, h_hbm, o_ref, buf, sem, *, tg):
    i = pl.program_id(0)
    nact = nact_ref[0]
    slot = i % 2

    def start(blk, s):
        _start_row_copies(h_hbm, src_ref, blk * tg, buf.at[s], sem.at[s], tg)

    @pl.when(jnp.logical_and(i == 0, nact > 0))
    def _():
        start(0, 0)

    @pl.when(i + 1 < nact)
    def _():
        start(i + 1, 1 - slot)

    @pl.when(i < nact)
    def _():
        _wait_row_copies(h_hbm, buf.at[slot], sem.at[slot], tg)
        o_ref[...] = buf[slot].astype(o_ref.dtype)

    @pl.when(i >= nact)
    def _():
        o_ref[...] = jnp.zeros(o_ref.shape, o_ref.dtype)


def gather_call(h, src_token, n_active_blocks, *, n_slots, tg=256):
    T, D = h.shape
    grid_spec = pltpu.PrefetchScalarGridSpec(
        num_scalar_prefetch=2,
        grid=(n_slots // tg,),
        in_specs=[pl.BlockSpec(memory_space=pl.ANY)],
        out_specs=pl.BlockSpec((tg, D), lambda i, src, nact: (i, 0)),
        scratch_shapes=[pltpu.VMEM((2, tg, D), F32), pltpu.SemaphoreType.DMA((2,))],
    )
    return pl.pallas_call(
        functools.partial(_gather_kernel, tg=tg),
        out_shape=jax.ShapeDtypeStruct((n_slots, D), BF16),
        grid_spec=grid_spec,
        compiler_params=_cparams(("arbitrary",)), name="gather",
    )(src_token, n_active_blocks, h)


def _combine_kernel(s1_ref, s2_ref, y_hbm, g1_ref, g2_ref, x_ref, gpost_ref, gtp, gts, o_ref,
                    buf, sem, *, tc, n_tiles, n_ptiles):
    i = pl.program_id(0)
    slot = i % 2

    def start(t, s):
        _start_row_copies(y_hbm, s1_ref, t * tc, buf.at[s, 0], sem.at[s, 0], tc)
        _start_row_copies(y_hbm, s2_ref, t * tc, buf.at[s, 1], sem.at[s, 1], tc)

    @pl.when(i == 0)
    def _():
        start(0, 0)

    @pl.when(i + 1 < n_tiles)
    def _():
        start(i + 1, 1 - slot)

    _wait_row_copies(y_hbm, buf.at[slot, 0], sem.at[slot, 0], tc)
    _wait_row_copies(y_hbm, buf.at[slot, 1], sem.at[slot, 1], tc)
    f = g1_ref[...] * buf[slot, 0] + g2_ref[...] * buf[slot, 1]
    o_ref[...] = x_ref[...] + _mod_value(i, n_ptiles, gtp, gts) * _rms(f, gpost_ref[...])


def combine_call(geom, y, slot1, slot2, g1, g2, x, g_post, modp, mods, l, gt_col):
    T, D = x.shape
    tc = geom.tm
    row = pl.BlockSpec((tc, D), lambda i, *_: (i, 0))
    col = pl.BlockSpec((tc, 1), lambda i, *_: (i, 0))
    grid_spec = pltpu.PrefetchScalarGridSpec(
        num_scalar_prefetch=2,
        grid=(geom.n_tiles,),
        in_specs=[pl.BlockSpec(memory_space=pl.ANY), col, col, row,
                  pl.BlockSpec((None, 1, D), lambda i, *_: (l, 0, 0))]
                 + _mod_specs(geom, D, l, gt_col),
        out_specs=row,
        scratch_shapes=[pltpu.VMEM((2, 2, tc, D), F32), pltpu.SemaphoreType.DMA((2, 2))],
    )
    return pl.pallas_call(
        functools.partial(_combine_kernel, tc=tc, n_tiles=geom.n_tiles, n_ptiles=geom.n_ptiles),
        out_shape=jax.ShapeDtypeStruct((T, D), F32),
        grid_spec=grid_spec,
        compiler_params=_cparams(("arbitrary",)), name="combine",
    )(slot1, slot2, y, g1.reshape(T, 1), g2.reshape(T, 1), x, g_post, modp, mods)


def moe_dispatch_plan(e1, e2, r1, r2, counts, *, tm, n_tiles_max):
    T = e1.shape[0]
    tiles_e = (counts + tm - 1) // tm
    tile_end = jnp.cumsum(tiles_e)
    tile_off = tile_end - tiles_e
    n_active = tile_end[-1]
    slot1 = tile_off[e1] * tm + r1
    slot2 = tile_off[e2] * tm + r2
    tok = jnp.arange(T, dtype=I32)
    src = jnp.zeros((n_tiles_max * tm,), I32).at[slot1].set(tok).at[slot2].set(tok)
    t = jnp.arange(n_tiles_max, dtype=I32)
    tile_x = jnp.minimum(t, n_active - 1).astype(I32)
    tile_expert = jnp.minimum(jnp.sum(tile_end[None, :] <= tile_x[:, None], axis=1), N_EXPERTS - 1).astype(I32)
    tile_active = (t < n_active).astype(I32)
    return slot1.astype(I32), slot2.astype(I32), src, tile_expert, tile_x, tile_active, n_active.astype(I32)


def kernel(x_prompt, x_sample, c_prompt, c_sample, cache_k, cache_v, page_table, g_pre_mix, g_post_mix, g_pre_ffn, g_post_ffn, w_ada, b_ada, w_in, w_out, lam_q1, lam_k1, lam_q2, lam_k2, g_subln, gm_ws, gm_b, gm_norm_g, gm_norm_b, w_d_gate, w_d_up, w_d_down, w_router, w_e_gate, w_e_up, w_e_down):
    B, S, D = x_prompt.shape
    Bd, Td, _ = x_sample.shape
    L = w_in.shape[0]
    NP, NS = B * S, Bd * Td
    T = NP + NS
    H = DA_HEADS
    QKW = 2 * DA_HEAD_DIM * H
    VW = DA_VDIM * H
    GW = GM_HEADS * GM_DIM
    assert S % CHUNK == 0 and NS % CHUNK == 0 and CHUNK % Td == 0

    tm_norm = min(256, NS)
    geom = RowGeom(B, S, Bd, Td, tm_norm)
    tm_ffn = min(1024, NS)
    assert T % tm_ffn == 0

    x = jnp.concatenate([x_prompt.reshape(NP, D), x_sample.reshape(NS, D)], axis=0)
    n_c = B + Bd
    c_all = jnp.concatenate([c_prompt, c_sample, jnp.zeros((-n_c % 8, D), F32)], axis=0)
    mod = ada_call(c_all, w_ada, b_ada)
    modp = mod[:, :B].reshape(L, B, 1, 6 * D)
    mods = jnp.repeat(mod[:, B:n_c], Td, axis=1)
    SH_M, SC_M, GT_M, SH_F, SC_F, GT_F = range(6)

    g3 = lambda a: a.reshape(L, 1, a.shape[-1])
    g_pre_mix3, g_post_mix3, g_pre_ffn3, g_post_ffn3 = map(g3, (g_pre_mix, g_post_mix, g_pre_ffn, g_post_ffn))
    g_subln3 = g3(g_subln)
    ng3 = gm_norm_g.reshape(L, 1, GW)
    nb3 = gm_norm_b.reshape(L, 1, GW)
    lamv = jnp.stack([lam_q1, lam_k1, lam_q2, lam_k2], axis=1)

    n_dense_tiles = T // tm_ffn
    dense_te = jnp.zeros((n_dense_tiles,), I32)
    dense_xi = jnp.arange(n_dense_tiles, dtype=I32)
    dense_act = jnp.ones((n_dense_tiles,), I32)

    (h,) = norm_call(geom, x, modp, mods, g_pre=g_pre_mix3, l_pre=0, sc_col=SC_M, sh_col=SH_M)

    k_p, v_p, k_s, v_s, gv_s = [], [], [], [], []
    for l in range(L):
        lam_init = 0.8 - 0.6 * math.exp(-0.3 * l)
        z = mm_call([h], w_in, l)
        k_p.append(z[:NP, QKW:2 * QKW].reshape(B, S, H, 2, DA_HEAD_DIM))
        v_p.append(z[:NP, 2 * QKW:2 * QKW + VW].reshape(B, S, H, DA_VDIM))
        k_s.append(z[NP:, QKW:2 * QKW].reshape(Bd, Td, H, 2, DA_HEAD_DIM))
        v_s.append(z[NP:, 2 * QKW:2 * QKW + VW].reshape(Bd, Td, H, DA_VDIM))

        o_att_p = attn_prompt_call(z, lamv, g_subln.reshape(L, DA_VDIM, 1), l, n_batch=B, seq=S,
                                   lam_init=lam_init)
        o_att_s = attn_sample_call(z, cache_k, cache_v, page_table, lamv, g_subln3, l,
                                   row0=NP, dec_len=Td, lam_init=lam_init)
        o_att = jnp.concatenate([o_att_p, o_att_s], axis=0)

        reps = CHUNK // Td
        wmix = jnp.stack([gm_ws[l], jnp.tile(gm_ws[l][:, :Td, :Td], (1, reps, reps))])
        bmix = jnp.stack([gm_b[l].T, jnp.tile(gm_b[l][:, :Td], (1, reps)).T])
        o_gm, gvn = chunk_mlp_call(z, wmix, bmix, ng3, nb3, l, n_ptiles=NP // CHUNK, dec_len=Td,
                                   col0=2 * QKW + VW)
        gv_s.append(gvn[NP:].reshape(Bd, Td, GM_HEADS, GM_DIM))

        m = mm_call([o_att, o_gm], w_out, l)
        is_moe = l % 2 == 1
        x, h2 = norm_call(geom, x, modp, mods, m=m, g_post=g_post_mix3, l_post=l, gt_col=GT_M,
                          g_pre=g_pre_ffn3, l_pre=l, sc_col=SC_F, sh_col=SH_F,
                          h_dtype=F32 if is_moe else BF16)
        wl = l // 2
        if not is_moe:
            f = ffn_call(h2, w_d_gate[:, None], w_d_up[:, None], w_d_down[:, None], wl,
                         dense_te, dense_xi, dense_act, tm=tm_ffn)
            if l + 1 < L:
                x, h = norm_call(geom, x, modp, mods, m=f, g_post=g_post_ffn3, l_post=l, gt_col=GT_F,
                                 g_pre=g_pre_mix3, l_pre=l + 1, sc_col=SC_M, sh_col=SH_M)
            else:
                (x,) = norm_call(geom, x, modp, mods, m=f, g_post=g_post_ffn3, l_post=l, gt_col=GT_F)
        else:
            e1, e2, r1, r2, g1, g2, counts = router_call(h2, w_router, wl)
            n_tiles_max = (2 * T + N_EXPERTS * (tm_ffn - 1)) // tm_ffn
            slot1, slot2, src, te, xi, act, n_active = moe_dispatch_plan(
                e1, e2, r1, r2, counts, tm=tm_ffn, n_tiles_max=n_tiles_max)
            tg = min(256, tm_ffn)
            xs = gather_call(h2, src, (n_active * (tm_ffn // tg)).reshape(1),
                             n_slots=n_tiles_max * tm_ffn, tg=tg)
            y = ffn_call(xs, w_e_gate, w_e_up, w_e_down, wl, te, xi, act, tm=tm_ffn)
            x = combine_call(geom, y, slot1, slot2, g1, g2, x, g_post_ffn3, modp, mods, l, GT_F)
            if l + 1 < L:
                (h,) = norm_call(geom, x, modp, mods, g_pre=g_pre_mix3, l_pre=l + 1, sc_col=SC_M, sh_col=SH_M)

    y_prompt = x[:NP].reshape(B, S, D)
    y_sample = x[NP:].reshape(Bd, Td, D)
    return (y_prompt, y_sample, jnp.stack(k_p), jnp.stack(v_p), jnp.stack(k_s), jnp.stack(v_s),
            jnp.stack(gv_s))
```

```python
import functools
import math

import jax
import jax.numpy as jnp
from jax import lax
from jax.experimental import pallas as pl
from jax.experimental.pallas import tpu as pltpu

F32 = jnp.float32
BF16 = jnp.bfloat16
I32 = jnp.int32

EPS = 1e-6
DA_HEADS = 8
DA_HEAD_DIM = 64
DA_VDIM = 128
GM_HEADS = 8
GM_DIM = 128
CHUNK = 128
PAGE_SIZE = 128
N_EXPERTS = 8
NEG = -1e30
LOG2E = 1.4426950408889634

V7X_VMEM_BYTES = 64 * 1024 * 1024
VMEM_LIMIT = V7X_VMEM_BYTES - 8 * 1024 * 1024

NT_DIMS = (((1,), (1,)), ((), ()))

LANES = 128


def _rows_2d(ref3, *lead):
    return jnp.concatenate([ref3[(*lead, slice(None), j, slice(None))] for j in range(ref3.shape[-2])], axis=1)


KEY_PIECE = 256
QUERY_PIECE = 128


def _cparams(sem):
    return pltpu.CompilerParams(dimension_semantics=sem, vmem_limit_bytes=VMEM_LIMIT)


def _fit_tile(n, t):
    while n % t:
        t //= 2
    return t


def _rms(x, g):
    return x * lax.rsqrt(jnp.mean(x * x, axis=-1, keepdims=True) + EPS) * g


def _ada_kernel(c_ref, w_ref, b_ref, o_ref):
    c = c_ref[...]
    a = (c * jax.nn.sigmoid(c)).astype(BF16)
    o_ref[...] = jnp.dot(a, w_ref[...].astype(BF16), preferred_element_type=F32) + b_ref[...]


def ada_call(c_all, w_ada, b_ada):
    R, D = c_all.shape
    L, _, N = w_ada.shape
    tn = min(1024, N)
    return pl.pallas_call(
        _ada_kernel,
        out_shape=jax.ShapeDtypeStruct((L, R, N), F32),
        grid=(L, N // tn),
        in_specs=[
            pl.BlockSpec((R, D), lambda l, j: (0, 0)),
            pl.BlockSpec((None, D, tn), lambda l, j: (l, 0, j)),
            pl.BlockSpec((None, 1, tn), lambda l, j: (l, 0, j)),
        ],
        out_specs=pl.BlockSpec((None, R, tn), lambda l, j: (l, 0, j)),
        compiler_params=_cparams(("arbitrary", "arbitrary")),
        name="ada",
    )(c_all, w_ada, b_ada.reshape(L, 1, N))


class RowGeom:
    def __init__(self, n_batch, seq, n_dec, dec_len, tm):
        assert seq % tm == 0 and (n_dec * dec_len) % tm == 0
        self.tm = tm
        self.n_batch = n_batch
        self.tiles_per_batch = seq // tm
        self.n_ptiles = n_batch * self.tiles_per_batch
        self.n_tiles = self.n_ptiles + (n_dec * dec_len) // tm


def _mod_specs(geom, D, l, col):
    tpb, npt, nb = geom.tiles_per_batch, geom.n_ptiles, geom.n_batch
    sp = pl.BlockSpec((None, None, 1, D),
                      lambda i, *_: (l, jnp.minimum(i // tpb, nb - 1), 0, col))
    ss = pl.BlockSpec((None, geom.tm, D),
                      lambda i, *_: (l, jnp.maximum(i - npt, 0), col))
    return [sp, ss]


def _mod_value(i, n_ptiles, p_ref, s_ref):
    return jnp.where(i < n_ptiles, p_ref[...], s_ref[...])


def _norm_kernel(*refs, n_ptiles, post, pre):
    it = iter(refs)
    x_ref = next(it)
    if post:
        m_ref, gpost_ref, gtp, gts = next(it), next(it), next(it), next(it)
    if pre:
        gpre_ref, scp, scs, shp, shs = next(it), next(it), next(it), next(it), next(it)
    if post:
        xo_ref = next(it)
    if pre:
        h_ref = next(it)
    i = pl.program_id(0)
    x = x_ref[...]
    if post:
        x = x + _mod_value(i, n_ptiles, gtp, gts) * _rms(m_ref[...], gpost_ref[...])
        xo_ref[...] = x
    if pre:
        h = _rms(x, gpre_ref[...]) * (1.0 + _mod_value(i, n_ptiles, scp, scs)) \
            + _mod_value(i, n_ptiles, shp, shs)
        h_ref[...] = h.astype(h_ref.dtype).reshape(h_ref.shape)


def norm_call(geom, x, modp, mods, *, m=None, g_post=None, l_post=None, gt_col=None,
              g_pre=None, l_pre=None, sc_col=None, sh_col=None, h_dtype=BF16, h_row_major=False):
    T, D = x.shape
    tm = geom.tm
    post, pre = m is not None, g_pre is not None
    row = pl.BlockSpec((tm, D), lambda i: (i, 0))
    args, specs = [x], [row]
    if post:
        args += [m, g_post, modp, mods]
        specs += [row, pl.BlockSpec((None, 1, D), lambda i: (l_post, 0, 0))]
        specs += _mod_specs(geom, D, l_post, gt_col)
    if pre:
        args += [g_pre, modp, mods, modp, mods]
        specs += [pl.BlockSpec((None, 1, D), lambda i: (l_pre, 0, 0))]
        specs += _mod_specs(geom, D, l_pre, sc_col) + _mod_specs(geom, D, l_pre, sh_col)
    out_shape, out_specs = [], []
    if post:
        out_shape.append(jax.ShapeDtypeStruct((T, D), F32))
        out_specs.append(row)
    if pre and h_row_major:
        out_shape.append(jax.ShapeDtypeStruct((T, D // LANES, LANES), h_dtype))
        out_specs.append(pl.BlockSpec((tm, D // LANES, LANES), lambda i: (i, 0, 0)))
    elif pre:
        out_shape.append(jax.ShapeDtypeStruct((T, D), h_dtype))
        out_specs.append(row)
    outs = pl.pallas_call(
        functools.partial(_norm_kernel, n_ptiles=geom.n_ptiles, post=post, pre=pre),
        out_shape=out_shape, grid=(geom.n_tiles,), in_specs=specs, out_specs=out_specs,
        compiler_params=_cparams(("arbitrary",)), name="norm",
    )(*args)
    return outs


def _mm_kernel(*refs, split, n_head_tiles):
    it = iter(refs)
    a_refs = [(next(it), next(it)) if s else next(it) for s in split]
    w_refs = [next(it) for _ in split]
    o_ref = next(it)
    i = pl.program_id(0)
    acc = None
    for a_ref, w_ref in zip(a_refs, w_refs):
        a = jnp.where(i < n_head_tiles, a_ref[0][...], a_ref[1][...]) if isinstance(a_ref, tuple) else a_ref[...]
        d = jnp.dot(a.astype(BF16), w_ref[...].astype(BF16), preferred_element_type=F32)
        acc = d if acc is None else acc + d
    o_ref[...] = acc


def mm_call(a_list, w, l, *, tm=1024, tn=512):
    split = [isinstance(a, tuple) for a in a_list]
    rows = lambda a: a[0].shape[0] + a[1].shape[0] if isinstance(a, tuple) else a.shape[0]
    width = lambda a: a[0].shape[1] if isinstance(a, tuple) else a.shape[1]
    T = rows(a_list[0])
    N = w.shape[2]
    tm, tn = _fit_tile(T, tm), _fit_tile(N, tn)
    n_head_tiles = 0
    args, specs = [], []
    for a in a_list:
        if isinstance(a, tuple):
            tm = _fit_tile(a[1].shape[0], _fit_tile(a[0].shape[0], tm))
    for a in a_list:
        if isinstance(a, tuple):
            nh = a[0].shape[0] // tm
            assert n_head_tiles in (0, nh)
            n_head_tiles = nh
            args += [a[0], a[1]]
            specs += [pl.BlockSpec((tm, width(a)), lambda i, j, nh=nh: (jnp.minimum(i, nh - 1), 0)),
                      pl.BlockSpec((tm, width(a)), lambda i, j, nh=nh: (jnp.maximum(i - nh, 0), 0))]
        else:
            args.append(a)
            specs.append(pl.BlockSpec((tm, width(a)), lambda i, j: (i, 0)))
    for k, a in enumerate(a_list):
        specs.append(pl.BlockSpec((None, width(a), tn), lambda i, j, k=k: (l, k, j)))
    return pl.pallas_call(
        functools.partial(_mm_kernel, split=tuple(split), n_head_tiles=n_head_tiles),
        out_shape=jax.ShapeDtypeStruct((T, N), F32),
        grid=(T // tm, N // tn), in_specs=specs,
        out_specs=pl.BlockSpec((tm, tn), lambda i, j: (i, j)),
        compiler_params=_cparams(("arbitrary", "arbitrary")), name="matmul",
    )(*args, *([w] * len(a_list)))


def _lam_value(lamv_ref, lam_init):
    v = lamv_ref[...]
    s1 = jnp.sum(v[0:1] * v[1:2], axis=-1, keepdims=True)
    s2 = jnp.sum(v[2:3] * v[3:4], axis=-1, keepdims=True)
    return jnp.exp(s1) - jnp.exp(s2) + lam_init


def _alibi_slope(head):
    return lax.bitcast_convert_type((126 - head) << 23, F32)


def _attn_p_kernel(qi_tab, ki_tab, lamv_ref, q_ref, k_ref, v_ref, g_ref, o_ref,
                   q_s, m_s, l_s, acc_s, *, tq, lam_init):
    h = pl.program_id(1)
    p = pl.program_id(2)
    qi, ki = qi_tab[p], ki_tab[p]

    @pl.when(ki == 0)
    def _():
        q = q_ref[...] * (DA_HEAD_DIM ** -0.5 * LOG2E)
        lane = lax.broadcasted_iota(I32, q.shape, 1)
        q_s[0] = jnp.where(lane < DA_HEAD_DIM, q, 0.0).astype(BF16)
        q_s[1] = jnp.where(lane >= DA_HEAD_DIM, q, 0.0).astype(BF16)
        m_s[...] = jnp.full(m_s.shape, NEG, F32)
        l_s[...] = jnp.zeros(l_s.shape, F32)
        acc_s[...] = jnp.zeros(acc_s.shape, F32)

    def update(masked):
        kb, qb = min(KEY_PIECE, tq), min(QUERY_PIECE, tq)
        k = k_ref[...].astype(BF16)
        vt = v_ref[...].T.astype(BF16)
        slope = _alibi_slope(jnp.zeros((kb, qb), I32) + h) * LOG2E
        krows = [lax.broadcasted_iota(I32, (kb, qb), 0) + k0 for k0 in range(0, tq, kb)]
        biases = [slope * ((ki - qi) * tq + kr).astype(F32) for kr in krows]
        for q0 in range(0, tq, qb):
            cols = slice(q0, q0 + qb)
            qcol = lax.broadcasted_iota(I32, (kb, qb), 1) + q0
            for mi in range(2):
                m, l, acc = m_s[mi, :, cols], l_s[mi, :, cols], acc_s[mi, :, cols]
                for kp, k0 in enumerate(range(0, tq, kb)):
                    if masked and k0 > q0 + qb - 1:
                        continue
                    b = biases[kp]
                    if masked and k0 + kb - 1 > q0:
                        b = jnp.where(krows[kp] <= qcol, b, NEG)
                    s = lax.dot_general(k[k0:k0 + kb], q_s[mi, cols, :], NT_DIMS,
                                        preferred_element_type=F32) + b
                    m_new = jnp.maximum(m, jnp.max(s, axis=0, keepdims=True))
                    alpha = jnp.exp2(m - m_new)
                    pm = jnp.exp2(s - m_new)
                    l = alpha * l + jnp.sum(pm, axis=0, keepdims=True)
                    acc = alpha * acc + jnp.dot(vt[:, k0:k0 + kb], pm.astype(BF16),
                                                preferred_element_type=F32)
                    m = m_new
                m_s[mi, :, cols], l_s[mi, :, cols], acc_s[mi, :, cols] = m, l, acc

    @pl.when(ki < qi)
    def _():
        update(False)

    @pl.when(ki == qi)
    def _():
        update(True)
        lam = _lam_value(lamv_ref, lam_init)
        o = acc_s[0] / l_s[0] - lam * (acc_s[1] / l_s[1])
        ms = jnp.mean(o * o, axis=0, keepdims=True)
        on = o * lax.rsqrt(ms + EPS) * g_ref[...] * (1.0 - lam_init)
        o_ref[...] = on.T


def attn_prompt_call(z, lamv, g_subln_col, l, *, n_batch, seq, lam_init, tq=512):
    tq = min(tq, seq)
    nq = seq // tq
    pairs = [(qi, ki) for qi in range(nq) for ki in range(qi + 1)]
    qi_tab = jnp.asarray([p[0] for p in pairs], I32)
    ki_tab = jnp.asarray([p[1] for p in pairs], I32)
    H = DA_HEADS
    w = 2 * DA_HEAD_DIM
    grid_spec = pltpu.PrefetchScalarGridSpec(
        num_scalar_prefetch=2,
        grid=(n_batch, H, len(pairs)),
        in_specs=[
            pl.BlockSpec((None, 4, DA_HEAD_DIM), lambda b, h, p, qt, kt: (l, 0, 0)),
            pl.BlockSpec((tq, w), lambda b, h, p, qt, kt: (b * nq + qt[p], h)),
            pl.BlockSpec((tq, w), lambda b, h, p, qt, kt: (b * nq + kt[p], H + h)),
            pl.BlockSpec((tq, DA_VDIM), lambda b, h, p, qt, kt: (b * nq + kt[p], 2 * H + h)),
            pl.BlockSpec((None, DA_VDIM, 1), lambda b, h, p, qt, kt: (l, 0, 0)),
        ],
        out_specs=pl.BlockSpec((tq, DA_VDIM), lambda b, h, p, qt, kt: (b * nq + qt[p], h)),
        scratch_shapes=[
            pltpu.VMEM((2, tq, w), BF16),
            pltpu.VMEM((2, 1, tq), F32),
            pltpu.VMEM((2, 1, tq), F32),
            pltpu.VMEM((2, DA_VDIM, tq), F32),
        ],
    )
    return pl.pallas_call(
        functools.partial(_attn_p_kernel, tq=tq, lam_init=lam_init),
        out_shape=jax.ShapeDtypeStruct((n_batch * seq, H * DA_VDIM), F32),
        grid_spec=grid_spec,
        compiler_params=_cparams(("arbitrary", "arbitrary", "arbitrary")),
        name="attn_prompt",
    )(qi_tab, ki_tab, lamv, z, z, z, g_subln_col)


def _attn_s_kernel(pt_ref, lamv_ref, q_ref, kn_ref, vn_ref, g_ref, ckt_hbm, cv_hbm, o_ref,
                   kbuf, vbuf, sem, *, layer, n_dec, n_pages, dec_len, lam_init):
    H, W = DA_HEADS, 2 * DA_HEAD_DIM * DA_HEADS
    R = 2 * H * dec_len
    past = n_pages * PAGE_SIZE
    b = pl.program_id(0)
    slot = b % 2

    def start_pages(seq, s):
        for pg in range(n_pages):
            page = pt_ref[seq * n_pages + pg]
            pltpu.make_async_copy(ckt_hbm.at[layer, page], kbuf.at[s, pg], sem.at[s, 0]).start()
            pltpu.make_async_copy(cv_hbm.at[layer, page], vbuf.at[s, pg], sem.at[s, 1]).start()

    @pl.when(b == 0)
    def _():
        start_pages(0, 0)

    @pl.when(b + 1 < n_dec)
    def _():
        start_pages(b + 1, 1 - slot)

    pltpu.make_async_copy(ckt_hbm.at[layer, pl.ds(0, n_pages)], kbuf.at[slot], sem.at[slot, 0]).wait()
    pltpu.make_async_copy(cv_hbm.at[layer, pl.ds(0, n_pages)], vbuf.at[slot], sem.at[slot, 1]).wait()

    q = q_ref[...] * (DA_HEAD_DIM ** -0.5 * LOG2E)
    qt = jnp.broadcast_to(q[None], (2 * H, dec_len, W)).reshape(R, W)
    r = lax.broadcasted_iota(I32, (R, W), 0)
    c = lax.broadcasted_iota(I32, (R, W), 1)
    row_map, row_head = r // (H * dec_len), (r // dec_len) % H
    col_head, col_map = c // (2 * DA_HEAD_DIM), (c // DA_HEAD_DIM) % 2
    keep = jnp.logical_and(row_map == col_map, row_head == col_head)
    qx = jnp.where(keep, qt, 0.0).astype(BF16)
    lane = lax.broadcasted_iota(I32, (R, PAGE_SIZE), 1)
    rr = lax.broadcasted_iota(I32, (R, PAGE_SIZE), 0)
    slope = _alibi_slope((rr // dec_len) % H) * LOG2E
    tiles = []
    for pg in range(n_pages):
        kt = kbuf[slot, pg].astype(BF16)
        bias = slope * (lane + (pg * PAGE_SIZE - past)).astype(F32)
        tiles.append(jnp.dot(qx, kt, preferred_element_type=F32) + bias)
    zpad = jnp.zeros((PAGE_SIZE - dec_len, W), F32)
    kn = jnp.concatenate([kn_ref[...], zpad], axis=0).astype(BF16)
    s_new = lax.dot_general(qx, kn, NT_DIMS, preferred_element_type=F32) + slope * lane.astype(F32)
    tiles.append(jnp.where(lane <= rr % dec_len, s_new, NEG))

    m = tiles[0]
    for t in tiles[1:]:
        m = jnp.maximum(m, t)
    m = jnp.max(m, axis=-1, keepdims=True)
    ps = [jnp.exp2(t - m) for t in tiles]
    lsum = ps[0]
    for t in ps[1:]:
        lsum = lsum + t
    inv = 1.0 / jnp.sum(lsum, axis=-1, keepdims=True)

    lam = _lam_value(lamv_ref, lam_init)
    hq = H * dec_len
    for h in range(H):
        rows1 = slice(h * dec_len, (h + 1) * dec_len)
        rows2 = slice(hq + h * dec_len, hq + (h + 1) * dec_len)
        cols = slice(h * DA_VDIM, (h + 1) * DA_VDIM)
        ph = jnp.concatenate(
            [jnp.concatenate([t[rows1], t[rows2]], axis=0).astype(BF16) for t in ps], axis=1)
        vparts = [vbuf[slot, pg, pl.ds(h, PAGE_SIZE, stride=H), :].astype(BF16) for pg in range(n_pages)]
        vparts.append(jnp.concatenate([vn_ref[:, cols], zpad[:, :DA_VDIM]], axis=0).astype(BF16))
        acc = jnp.dot(ph, jnp.concatenate(vparts, axis=0), preferred_element_type=F32)
        o = acc[:dec_len] * inv[rows1] - lam * (acc[dec_len:] * inv[rows2])
        o_ref[:, cols] = _rms(o, g_ref[...]) * (1.0 - lam_init)


def attn_sample_call(z, cache_k, cache_v, page_table, lamv, g_subln, l, *, row0, dec_len, lam_init):
    n_dec, n_pages = page_table.shape
    H = DA_HEADS
    W = 2 * DA_HEAD_DIM * H
    n_layers, n_pool = cache_k.shape[:2]
    ckt = jnp.transpose(cache_k, (0, 1, 3, 4, 5, 2)).reshape(n_layers, n_pool, W, PAGE_SIZE)
    cv = cache_v.reshape(n_layers, n_pool, PAGE_SIZE * H, DA_VDIM)
    rb = row0 // dec_len
    grid_spec = pltpu.PrefetchScalarGridSpec(
        num_scalar_prefetch=1,
        grid=(n_dec,),
        in_specs=[
            pl.BlockSpec((None, 4, DA_HEAD_DIM), lambda b, pt: (l, 0, 0)),
            pl.BlockSpec((dec_len, W), lambda b, pt: (rb + b, 0)),
            pl.BlockSpec((dec_len, W), lambda b, pt: (rb + b, 1)),
            pl.BlockSpec((dec_len, W), lambda b, pt: (rb + b, 2)),
            pl.BlockSpec((None, 1, DA_VDIM), lambda b, pt: (l, 0, 0)),
            pl.BlockSpec(memory_space=pl.ANY),
            pl.BlockSpec(memory_space=pl.ANY),
        ],
        out_specs=pl.BlockSpec((dec_len, W), lambda b, pt: (b, 0)),
        scratch_shapes=[
            pltpu.VMEM((2, n_pages, W, PAGE_SIZE), F32),
            pltpu.VMEM((2, n_pages, PAGE_SIZE * H, DA_VDIM), F32),
            pltpu.SemaphoreType.DMA((2, 2)),
        ],
    )
    return pl.pallas_call(
        functools.partial(_attn_s_kernel, layer=l, n_dec=n_dec, n_pages=n_pages, dec_len=dec_len,
                          lam_init=lam_init),
        out_shape=jax.ShapeDtypeStruct((n_dec * dec_len, W), F32),
        grid_spec=grid_spec,
        compiler_params=_cparams(("arbitrary",)),
        name="attn_sample",
    )(page_table.reshape(-1), lamv, z, z, z, g_subln, ckt, cv)


def _gm_kernel(gu_ref, gv_ref, w_ref, b_ref, ng_ref, nb_ref, o_ref, gvo_ref, *, n_ptiles, dec_len):
    i = pl.program_id(0)
    r = lax.broadcasted_iota(I32, (CHUNK, CHUNK), 0)
    c = lax.broadcasted_iota(I32, (CHUNK, CHUNK), 1)
    same_seq = jnp.where(r // dec_len == c // dec_len, 1, 0) + jnp.where(i < n_ptiles, 1, 0)
    keep = jnp.logical_and(c <= r, same_seq > 0)
    for g in range(GM_HEADS):
        cols = slice(g * GM_DIM, (g + 1) * GM_DIM)
        v = jax.nn.gelu(gv_ref[:, cols])
        mu = jnp.mean(v, axis=-1, keepdims=True)
        var = jnp.mean(jnp.square(v - mu), axis=-1, keepdims=True)
        vn = (v - mu) * lax.rsqrt(var + EPS) * ng_ref[:, cols] + nb_ref[:, cols]
        gvo_ref[:, cols] = vn
        wm = jnp.where(keep, w_ref[g], 0.0).astype(BF16)
        mixed = jnp.dot(wm, vn.astype(BF16), preferred_element_type=F32) + b_ref[:, g:g + 1]
        o_ref[:, cols] = jax.nn.gelu(gu_ref[:, cols]) * mixed


def chunk_mlp_call(z, wmix, bmix, ng, nb, l, *, n_ptiles, dec_len, col0):
    T = z.shape[0]
    GW = GM_HEADS * GM_DIM
    cb = col0 // GW
    row = pl.BlockSpec((CHUNK, GW), lambda i: (i, 0))
    sel = lambda i: jnp.where(i < n_ptiles, 0, 1)
    return pl.pallas_call(
        functools.partial(_gm_kernel, n_ptiles=n_ptiles, dec_len=dec_len),
        out_shape=[jax.ShapeDtypeStruct((T, GW), F32)] * 2,
        grid=(T // CHUNK,),
        in_specs=[
            pl.BlockSpec((CHUNK, GW), lambda i: (i, cb)),
            pl.BlockSpec((CHUNK, GW), lambda i: (i, cb + 1)),
            pl.BlockSpec((None, GM_HEADS, CHUNK, CHUNK), lambda i: (sel(i), 0, 0, 0)),
            pl.BlockSpec((None, CHUNK, GM_HEADS), lambda i: (sel(i), 0, 0)),
            pl.BlockSpec((None, 1, GW), lambda i: (l, 0, 0)),
            pl.BlockSpec((None, 1, GW), lambda i: (l, 0, 0)),
        ],
        out_specs=[row, row],
        compiler_params=_cparams(("arbitrary",)), name="chunk_mlp",
    )(z, z, wmix, bmix, ng, nb)


def _ffn_kernel(te_ref, xi_ref, nv_ref, x_ref, wg_ref, wu_ref, wd_ref, o_ref, *, n_sub):
    i, f = pl.program_id(0), pl.program_id(1)
    tm = x_ref.shape[0]
    sub = tm // n_sub
    nv = nv_ref[i]

    @pl.when(f == 0)
    def _():
        o_ref[...] = jnp.zeros(o_ref.shape, F32)

    def swiglu_rows(rows):
        x = x_ref[rows, :]
        g = jnp.dot(x, wg_ref[...].astype(BF16), preferred_element_type=F32)
        u = jnp.dot(x, wu_ref[...].astype(BF16), preferred_element_type=F32)
        a = (g * jax.nn.sigmoid(g) * u).astype(BF16)
        o_ref[rows, :] += jnp.dot(a, wd_ref[...].astype(BF16), preferred_element_type=F32)

    n_need = (nv + (sub - 1)) // sub

    @pl.when(n_need == n_sub)
    def _():
        swiglu_rows(slice(None))

    for sb in range(n_sub - 1):
        @pl.when(jnp.logical_and(n_need < n_sub, sb < n_need))
        def _():
            swiglu_rows(slice(sb * sub, (sb + 1) * sub))


def ffn_call(x, w_gate, w_up, w_down, wl, tile_expert, tile_x, tile_valid, *, tm, tf=256, n_sub=4):
    P, D = x.shape
    F = w_gate.shape[-1]
    tf = min(tf, F)
    nf = F // tf
    n_tiles = P // tm

    def fidx(i, f, act):
        return jnp.where(act[i] > 0, f, nf - 1)

    grid_spec = pltpu.PrefetchScalarGridSpec(
        num_scalar_prefetch=3,
        grid=(n_tiles, nf),
        in_specs=[
            pl.BlockSpec((tm, D), lambda i, f, te, xi, act: (xi[i], 0)),
            pl.BlockSpec((None, None, D, tf), lambda i, f, te, xi, act: (wl, te[i], 0, fidx(i, f, act))),
            pl.BlockSpec((None, None, D, tf), lambda i, f, te, xi, act: (wl, te[i], 0, fidx(i, f, act))),
            pl.BlockSpec((None, None, tf, D), lambda i, f, te, xi, act: (wl, te[i], fidx(i, f, act), 0)),
        ],
        out_specs=pl.BlockSpec((tm, D), lambda i, f, te, xi, act: (i, 0)),
    )
    return pl.pallas_call(
        functools.partial(_ffn_kernel, n_sub=n_sub),
        out_shape=jax.ShapeDtypeStruct((P, D), F32),
        grid_spec=grid_spec,
        compiler_params=_cparams(("arbitrary", "arbitrary")), name="ffn",
    )(tile_expert, tile_x, tile_valid, x, w_gate, w_up, w_down)


def _split3(x):
    hi = x.astype(BF16)
    r1 = x - hi.astype(F32)
    mid = r1.astype(BF16)
    lo = (r1 - mid.astype(F32)).astype(BF16)
    return hi, mid, lo


def _router_kernel(h_ref, wr_ref, tri_ref, idx_ref, gate_ref, cnt_ref, carry_s, *, tm):
    i = pl.program_id(0)

    @pl.when(i == 0)
    def _():
        carry_s[...] = jnp.zeros(carry_s.shape, F32)

    hs = _split3(_rows_2d(h_ref))
    ws = _split3(wr_ref[...])
    logits = jnp.zeros((N_EXPERTS, tm), F32)
    for a in range(3):
        for b in range(3 - a):
            logits += lax.dot_general(ws[a], hs[b], NT_DIMS, preferred_element_type=F32)
    eid = lax.broadcasted_iota(I32, (N_EXPERTS, tm), 0).astype(F32)
    none = float(N_EXPERTS)
    m1 = jnp.max(logits, axis=0, keepdims=True)
    e1 = jnp.min(jnp.where(logits == m1, eid, none), axis=0, keepdims=True)
    is1 = eid == e1
    rest = jnp.where(is1, -jnp.inf, logits)
    m2 = jnp.max(rest, axis=0, keepdims=True)
    e2 = jnp.min(jnp.where(rest == m2, eid, none), axis=0, keepdims=True)
    is2 = eid == e2
    ex = jnp.exp(m2 - m1)
    g1 = 1.0 / (1.0 + ex)
    g2 = ex / (1.0 + ex)
    sel = jnp.where(jnp.logical_or(is1, is2), 1.0, 0.0)
    before = jnp.dot(sel.astype(BF16), tri_ref[...], preferred_element_type=F32) + carry_s[...]
    r1 = jnp.sum(jnp.where(is1, before, 0.0), axis=0, keepdims=True)
    r2 = jnp.sum(jnp.where(is2, before, 0.0), axis=0, keepdims=True)
    carry_s[...] = carry_s[...] + jnp.sum(sel, axis=1, keepdims=True)
    idx_ref[0:1, :] = e1.astype(I32)
    idx_ref[1:2, :] = e2.astype(I32)
    idx_ref[2:3, :] = r1.astype(I32)
    idx_ref[3:4, :] = r2.astype(I32)
    idx_ref[4:8, :] = jnp.zeros((4, tm), I32)
    gate_ref[0:1, :] = g1
    gate_ref[1:2, :] = g2
    gate_ref[2:8, :] = jnp.zeros((6, tm), F32)
    cnt_ref[...] = jnp.broadcast_to(carry_s[...], cnt_ref.shape).astype(I32)


def router_call(h, w_router, wl, *, tm=512):
    T, D = h.shape[0], h.shape[1] * h.shape[2]
    tm = _fit_tile(T, tm)
    n = T // tm
    wr_t = jnp.swapaxes(w_router, 1, 2)
    tri = (jnp.arange(tm)[:, None] < jnp.arange(tm)[None, :]).astype(BF16)
    idx, gate, cnt = pl.pallas_call(
        functools.partial(_router_kernel, tm=tm),
        out_shape=[jax.ShapeDtypeStruct((n, 8, tm), I32),
                   jax.ShapeDtypeStruct((n, 8, tm), F32),
                   jax.ShapeDtypeStruct((N_EXPERTS, 128), I32)],
        grid=(n,),
        in_specs=[
            pl.BlockSpec((tm,) + h.shape[1:], lambda i: (i, 0, 0)),
            pl.BlockSpec((None, N_EXPERTS, D), lambda i: (wl, 0, 0)),
            pl.BlockSpec((tm, tm), lambda i: (0, 0)),
        ],
        out_specs=[
            pl.BlockSpec((None, 8, tm), lambda i: (i, 0, 0)),
            pl.BlockSpec((None, 8, tm), lambda i: (i, 0, 0)),
            pl.BlockSpec((N_EXPERTS, 128), lambda i: (0, 0)),
        ],
        scratch_shapes=[pltpu.VMEM((N_EXPERTS, 1), F32)],
        compiler_params=_cparams(("arbitrary",)), name="router",
    )(h, wr_t, tri)
    rows = lambda a, k: a[:, k, :].reshape(T)
    return (rows(idx, 0), rows(idx, 1), rows(idx, 2), rows(idx, 3),
            rows(gate, 0), rows(gate, 1), cnt[:, 0])


def _start_row_copies(src_hbm, idx_ref, base, dst, sem, n):
    def body(r, carry):
        pltpu.make_async_copy(src_hbm.at[pl.ds(idx_ref[base + r], 1)],
                              dst.at[pl.ds(r, 1)], sem).start()
        return carry
    lax.fori_loop(0, n, body, 0, unroll=8)


def _wait_row_copies(src_hbm, dst, sem, n):
    pltpu.make_async_copy(src_hbm.at[pl.ds(0, n)], dst, sem).wait()


def _gather_kernel(src_ref, nact_ref, h_hbm, o_ref, buf, sem, *, tg):
    i = pl.program_id(0)
    nact = nact_ref[0]
    slot = i % 2

    def start(blk, s):
        _start_row_copies(h_hbm, src_ref, blk * tg, buf.at[s], sem.at[s], tg)

    @pl.when(jnp.logical_and(i == 0, nact > 0))
    def _():
        start(0, 0)

    @pl.when(i + 1 < nact)
    def _():
        start(i + 1, 1 - slot)

    @pl.when(i < nact)
    def _():
        _wait_row_copies(h_hbm, buf.at[slot], sem.at[slot], tg)
        o_ref[...] = _rows_2d(buf, slot).astype(o_ref.dtype)

    @pl.when(i >= nact)
    def _():
        o_ref[...] = jnp.zeros(o_ref.shape, o_ref.dtype)


def gather_call(h, src_token, n_active_blocks, *, n_slots, tg=256):
    D = h.shape[1] * h.shape[2]
    grid_spec = pltpu.PrefetchScalarGridSpec(
        num_scalar_prefetch=2,
        grid=(n_slots // tg,),
        in_specs=[pl.BlockSpec(memory_space=pl.ANY)],
        out_specs=pl.BlockSpec((tg, D), lambda i, src, nact: (i, 0)),
        scratch_shapes=[pltpu.VMEM((2, tg) + h.shape[1:], F32), pltpu.SemaphoreType.DMA((2,))],
    )
    return pl.pallas_call(
        functools.partial(_gather_kernel, tg=tg),
        out_shape=jax.ShapeDtypeStruct((n_slots, D), BF16),
        grid_spec=grid_spec,
        compiler_params=_cparams(("arbitrary",)), name="gather",
    )(src_token, n_active_blocks, h)


def _combine_kernel(s1_ref, s2_ref, y_hbm, g1_ref, g2_ref, x_ref, gpost_ref, gtp, gts, o_ref,
                    buf, sem, *, tc, n_tiles, n_ptiles):
    i = pl.program_id(0)
    slot = i % 2

    def start(t, s):
        _start_row_copies(y_hbm, s1_ref, t * tc, buf.at[s, 0], sem.at[s, 0], tc)
        _start_row_copies(y_hbm, s2_ref, t * tc, buf.at[s, 1], sem.at[s, 1], tc)

    @pl.when(i == 0)
    def _():
        start(0, 0)

    @pl.when(i + 1 < n_tiles)
    def _():
        start(i + 1, 1 - slot)

    _wait_row_copies(y_hbm, buf.at[slot, 0], sem.at[slot, 0], tc)
    _wait_row_copies(y_hbm, buf.at[slot, 1], sem.at[slot, 1], tc)
    f = g1_ref[...] * buf[slot, 0] + g2_ref[...] * buf[slot, 1]
    o_ref[...] = x_ref[...] + _mod_value(i, n_ptiles, gtp, gts) * _rms(f, gpost_ref[...])


def combine_call(geom, y, slot1, slot2, g1, g2, x, g_post, modp, mods, l, gt_col):
    T, D = x.shape
    tc = geom.tm
    row = pl.BlockSpec((tc, D), lambda i, *_: (i, 0))
    col = pl.BlockSpec((tc, 1), lambda i, *_: (i, 0))
    grid_spec = pltpu.PrefetchScalarGridSpec(
        num_scalar_prefetch=2,
        grid=(geom.n_tiles,),
        in_specs=[pl.BlockSpec(memory_space=pl.ANY), col, col, row,
                  pl.BlockSpec((None, 1, D), lambda i, *_: (l, 0, 0))]
                 + _mod_specs(geom, D, l, gt_col),
        out_specs=row,
        scratch_shapes=[pltpu.VMEM((2, 2, tc, D), F32), pltpu.SemaphoreType.DMA((2, 2))],
    )
    return pl.pallas_call(
        functools.partial(_combine_kernel, tc=tc, n_tiles=geom.n_tiles, n_ptiles=geom.n_ptiles),
        out_shape=jax.ShapeDtypeStruct((T, D), F32),
        grid_spec=grid_spec,
        compiler_params=_cparams(("arbitrary",)), name="combine",
    )(slot1, slot2, y, g1.reshape(T, 1), g2.reshape(T, 1), x, g_post, modp, mods)


def moe_dispatch_plan(e1, e2, r1, r2, counts, *, tm, n_tiles_max):
    T = e1.shape[0]
    tiles_e = (counts + tm - 1) // tm
    tile_end = jnp.cumsum(tiles_e)
    tile_off = tile_end - tiles_e
    n_active = tile_end[-1]
    slot1 = tile_off[e1] * tm + r1
    slot2 = tile_off[e2] * tm + r2
    tok = jnp.arange(T, dtype=I32)
    src = jnp.zeros((n_tiles_max * tm,), I32).at[slot1].set(tok).at[slot2].set(tok)
    t = jnp.arange(n_tiles_max, dtype=I32)
    tile_x = jnp.minimum(t, n_active - 1).astype(I32)
    tile_expert = jnp.minimum(jnp.sum(tile_end[None, :] <= tile_x[:, None], axis=1), N_EXPERTS - 1).astype(I32)
    rows_left = counts[tile_expert] - (tile_x - tile_off[tile_expert]) * tm
    tile_valid = jnp.where(t < n_active, jnp.clip(rows_left, 0, tm), 0).astype(I32)
    return slot1.astype(I32), slot2.astype(I32), src, tile_expert, tile_x, tile_valid, n_active.astype(I32)


def kernel(x_prompt, x_sample, c_prompt, c_sample, cache_k, cache_v, page_table, g_pre_mix, g_post_mix, g_pre_ffn, g_post_ffn, w_ada, b_ada, w_in, w_out, lam_q1, lam_k1, lam_q2, lam_k2, g_subln, gm_ws, gm_b, gm_norm_g, gm_norm_b, w_d_gate, w_d_up, w_d_down, w_router, w_e_gate, w_e_up, w_e_down):
    B, S, D = x_prompt.shape
    Bd, Td, _ = x_sample.shape
    L = w_in.shape[0]
    NP, NS = B * S, Bd * Td
    T = NP + NS
    H = DA_HEADS
    QKW = 2 * DA_HEAD_DIM * H
    VW = DA_VDIM * H
    GW = GM_HEADS * GM_DIM
    assert S % CHUNK == 0 and NS % CHUNK == 0 and CHUNK % Td == 0

    tm_norm = min(256, NS)
    geom = RowGeom(B, S, Bd, Td, tm_norm)
    tm_ffn = min(1024, NS)
    assert T % tm_ffn == 0

    x = jnp.concatenate([x_prompt.reshape(NP, D), x_sample.reshape(NS, D)], axis=0)
    n_c = B + Bd
    c_all = jnp.concatenate([c_prompt, c_sample, jnp.zeros((-n_c % 8, D), F32)], axis=0)
    mod = ada_call(c_all, w_ada, b_ada)
    modp = mod[:, :B].reshape(L, B, 1, 6 * D)
    mods = jnp.repeat(mod[:, B:n_c], Td, axis=1)
    SH_M, SC_M, GT_M, SH_F, SC_F, GT_F = range(6)

    g3 = lambda a: a.reshape(L, 1, a.shape[-1])
    g_pre_mix3, g_post_mix3, g_pre_ffn3, g_post_ffn3 = map(g3, (g_pre_mix, g_post_mix, g_pre_ffn, g_post_ffn))
    g_subln3 = g3(g_subln)
    ng3 = gm_norm_g.reshape(L, 1, GW)
    nb3 = gm_norm_b.reshape(L, 1, GW)
    lamv = jnp.stack([lam_q1, lam_k1, lam_q2, lam_k2], axis=1)

    n_dense_tiles = T // tm_ffn
    dense_te = jnp.zeros((n_dense_tiles,), I32)
    dense_xi = jnp.arange(n_dense_tiles, dtype=I32)
    dense_act = jnp.full((n_dense_tiles,), tm_ffn, I32)

    (h,) = norm_call(geom, x, modp, mods, g_pre=g_pre_mix3, l_pre=0, sc_col=SC_M, sh_col=SH_M)

    k_p, v_p, k_s, v_s, gv_s = [], [], [], [], []
    for l in range(L):
        lam_init = 0.8 - 0.6 * math.exp(-0.3 * l)
        z = mm_call([h], w_in, l)
        k_p.append(z[:NP, QKW:2 * QKW].reshape(B, S, H, 2, DA_HEAD_DIM))
        v_p.append(z[:NP, 2 * QKW:2 * QKW + VW].reshape(B, S, H, DA_VDIM))
        k_s.append(z[NP:, QKW:2 * QKW].reshape(Bd, Td, H, 2, DA_HEAD_DIM))
        v_s.append(z[NP:, 2 * QKW:2 * QKW + VW].reshape(Bd, Td, H, DA_VDIM))

        o_att_p = attn_prompt_call(z, lamv, g_subln.reshape(L, DA_VDIM, 1), l, n_batch=B, seq=S,
                                   lam_init=lam_init)
        o_att_s = attn_sample_call(z, cache_k, cache_v, page_table, lamv, g_subln3, l,
                                   row0=NP, dec_len=Td, lam_init=lam_init)

        reps = CHUNK // Td
        wmix = jnp.stack([gm_ws[l], jnp.tile(gm_ws[l][:, :Td, :Td], (1, reps, reps))])
        bmix = jnp.stack([gm_b[l].T, jnp.tile(gm_b[l][:, :Td], (1, reps)).T])
        o_gm, gvn = chunk_mlp_call(z, wmix, bmix, ng3, nb3, l, n_ptiles=NP // CHUNK, dec_len=Td,
                                   col0=2 * QKW + VW)
        gv_s.append(gvn[NP:].reshape(Bd, Td, GM_HEADS, GM_DIM))

        m = mm_call([(o_att_p, o_att_s), o_gm], w_out, l)
        is_moe = l % 2 == 1
        x, h2 = norm_call(geom, x, modp, mods, m=m, g_post=g_post_mix3, l_post=l, gt_col=GT_M,
                          g_pre=g_pre_ffn3, l_pre=l, sc_col=SC_F, sh_col=SH_F,
                          h_dtype=F32 if is_moe else BF16, h_row_major=is_moe)
        wl = l // 2
        if not is_moe:
            f = ffn_call(h2, w_d_gate[:, None], w_d_up[:, None], w_d_down[:, None], wl,
                         dense_te, dense_xi, dense_act, tm=tm_ffn)
            if l + 1 < L:
                x, h = norm_call(geom, x, modp, mods, m=f, g_post=g_post_ffn3, l_post=l, gt_col=GT_F,
                                 g_pre=g_pre_mix3, l_pre=l + 1, sc_col=SC_M, sh_col=SH_M)
            else:
                (x,) = norm_call(geom, x, modp, mods, m=f, g_post=g_post_ffn3, l_post=l, gt_col=GT_F)
        else:
            e1, e2, r1, r2, g1, g2, counts = router_call(h2, w_router, wl)
            n_tiles_max = (2 * T + N_EXPERTS * (tm_ffn - 1)) // tm_ffn
            slot1, slot2, src, te, xi, act, n_active = moe_dispatch_plan(
                e1, e2, r1, r2, counts, tm=tm_ffn, n_tiles_max=n_tiles_max)
            tg = min(256, tm_ffn)
            xs = gather_call(h2, src, (n_active * (tm_ffn // tg)).reshape(1),
                             n_slots=n_tiles_max * tm_ffn, tg=tg)
            y = ffn_call(xs, w_e_gate, w_e_up, w_e_down, wl, te, xi, act, tm=tm_ffn)
            x = combine_call(geom, y, slot1, slot2, g1, g2, x, g_post_ffn3, modp, mods, l, GT_F)
            if l + 1 < L:
                (h,) = norm_call(geom, x, modp, mods, g_pre=g_pre_mix3, l_pre=l + 1, sc_col=SC_M, sh_col=SH_M)

    y_prompt = x[:NP].reshape(B, S, D)
    y_sample = x[NP:].reshape(Bd, Td, D)
    return (y_prompt, y_sample, jnp.stack(k_p), jnp.stack(v_p), jnp.stack(k_s), jnp.stack(v_s),
            jnp.stack(gv_s))
```

```python
import functools
import math

import jax
import jax.numpy as jnp
from jax import lax
from jax.experimental import pallas as pl
from jax.experimental.pallas import tpu as pltpu

F32 = jnp.float32
BF16 = jnp.bfloat16
I32 = jnp.int32

EPS = 1e-6
DA_HEADS = 8
DA_HEAD_DIM = 64
DA_VDIM = 128
GM_HEADS = 8
GM_DIM = 128
CHUNK = 128
PAGE_SIZE = 128
N_EXPERTS = 8
NEG = -1e30
LOG2E = 1.4426950408889634

V7X_VMEM_BYTES = 64 * 1024 * 1024
VMEM_LIMIT = V7X_VMEM_BYTES - 8 * 1024 * 1024

NT_DIMS = (((1,), (1,)), ((), ()))

LANES = 128


def _rows_2d(ref3, *lead):
    return jnp.concatenate([ref3[(*lead, slice(None), j, slice(None))] for j in range(ref3.shape[-2])], axis=1)


KEY_PIECE = 256
QUERY_PIECE = 128


def _cparams(sem):
    return pltpu.CompilerParams(dimension_semantics=sem, vmem_limit_bytes=VMEM_LIMIT)


def _fit_tile(n, t):
    while n % t:
        t //= 2
    return t


def _rms(x, g):
    return x * lax.rsqrt(jnp.mean(x * x, axis=-1, keepdims=True) + EPS) * g


def _ada_kernel(c_ref, w_ref, b_ref, o_ref):
    c = c_ref[...]
    a = (c * jax.nn.sigmoid(c)).astype(BF16)
    o_ref[...] = jnp.dot(a, w_ref[...].astype(BF16), preferred_element_type=F32) + b_ref[...]


def ada_call(c_all, w_ada, b_ada):
    R, D = c_all.shape
    L, _, N = w_ada.shape
    tn = min(1024, N)
    return pl.pallas_call(
        _ada_kernel,
        out_shape=jax.ShapeDtypeStruct((L, R, N), F32),
        grid=(L, N // tn),
        in_specs=[
            pl.BlockSpec((R, D), lambda l, j: (0, 0)),
            pl.BlockSpec((None, D, tn), lambda l, j: (l, 0, j)),
            pl.BlockSpec((None, 1, tn), lambda l, j: (l, 0, j)),
        ],
        out_specs=pl.BlockSpec((None, R, tn), lambda l, j: (l, 0, j)),
        compiler_params=_cparams(("arbitrary", "arbitrary")),
        name="ada",
    )(c_all, w_ada, b_ada.reshape(L, 1, N))


class RowGeom:
    def __init__(self, n_batch, seq, n_dec, dec_len, tm):
        assert seq % tm == 0 and (n_dec * dec_len) % tm == 0
        self.tm = tm
        self.n_batch = n_batch
        self.tiles_per_batch = seq // tm
        self.n_ptiles = n_batch * self.tiles_per_batch
        self.n_tiles = self.n_ptiles + (n_dec * dec_len) // tm


def _mod_specs(geom, D, l, col):
    tpb, npt, nb = geom.tiles_per_batch, geom.n_ptiles, geom.n_batch
    sp = pl.BlockSpec((None, None, 1, D),
                      lambda i, *_: (l, jnp.minimum(i // tpb, nb - 1), 0, col))
    ss = pl.BlockSpec((None, geom.tm, D),
                      lambda i, *_: (l, jnp.maximum(i - npt, 0), col))
    return [sp, ss]


def _mod_value(i, n_ptiles, p_ref, s_ref):
    return jnp.where(i < n_ptiles, p_ref[...], s_ref[...])


def _norm_kernel(*refs, n_ptiles, post, pre):
    it = iter(refs)
    x_ref = next(it)
    if post:
        m_ref, gpost_ref, gtp, gts = next(it), next(it), next(it), next(it)
    if pre:
        gpre_ref, scp, scs, shp, shs = next(it), next(it), next(it), next(it), next(it)
    if post:
        xo_ref = next(it)
    if pre:
        h_ref = next(it)
    i = pl.program_id(0)
    x = x_ref[...]
    if post:
        x = x + _mod_value(i, n_ptiles, gtp, gts) * _rms(m_ref[...], gpost_ref[...])
        xo_ref[...] = x
    if pre:
        h = _rms(x, gpre_ref[...]) * (1.0 + _mod_value(i, n_ptiles, scp, scs)) \
            + _mod_value(i, n_ptiles, shp, shs)
        h_ref[...] = h.astype(h_ref.dtype).reshape(h_ref.shape)


def norm_call(geom, x, modp, mods, *, m=None, g_post=None, l_post=None, gt_col=None,
              g_pre=None, l_pre=None, sc_col=None, sh_col=None, h_dtype=BF16, h_row_major=False):
    T, D = x.shape
    tm = geom.tm
    post, pre = m is not None, g_pre is not None
    row = pl.BlockSpec((tm, D), lambda i: (i, 0))
    args, specs = [x], [row]
    if post:
        args += [m, g_post, modp, mods]
        specs += [row, pl.BlockSpec((None, 1, D), lambda i: (l_post, 0, 0))]
        specs += _mod_specs(geom, D, l_post, gt_col)
    if pre:
        args += [g_pre, modp, mods, modp, mods]
        specs += [pl.BlockSpec((None, 1, D), lambda i: (l_pre, 0, 0))]
        specs += _mod_specs(geom, D, l_pre, sc_col) + _mod_specs(geom, D, l_pre, sh_col)
    out_shape, out_specs = [], []
    if post:
        out_shape.append(jax.ShapeDtypeStruct((T, D), F32))
        out_specs.append(row)
    if pre and h_row_major:
        out_shape.append(jax.ShapeDtypeStruct((T, D // LANES, LANES), h_dtype))
        out_specs.append(pl.BlockSpec((tm, D // LANES, LANES), lambda i: (i, 0, 0)))
    elif pre:
        out_shape.append(jax.ShapeDtypeStruct((T, D), h_dtype))
        out_specs.append(row)
    outs = pl.pallas_call(
        functools.partial(_norm_kernel, n_ptiles=geom.n_ptiles, post=post, pre=pre),
        out_shape=out_shape, grid=(geom.n_tiles,), in_specs=specs, out_specs=out_specs,
        compiler_params=_cparams(("arbitrary",)), name="norm",
    )(*args)
    return outs


def _mm_kernel(*refs, split, n_head_tiles):
    it = iter(refs)
    a_refs = [(next(it), next(it)) if s else next(it) for s in split]
    w_refs = [next(it) for _ in split]
    o_ref = next(it)
    i = pl.program_id(0)
    acc = None
    for a_ref, w_ref in zip(a_refs, w_refs):
        a = jnp.where(i < n_head_tiles, a_ref[0][...], a_ref[1][...]) if isinstance(a_ref, tuple) else a_ref[...]
        d = jnp.dot(a.astype(BF16), w_ref[...].astype(BF16), preferred_element_type=F32)
        acc = d if acc is None else acc + d
    o_ref[...] = acc


def mm_call(a_list, w, l, *, tm=1024, tn=512):
    split = [isinstance(a, tuple) for a in a_list]
    rows = lambda a: a[0].shape[0] + a[1].shape[0] if isinstance(a, tuple) else a.shape[0]
    width = lambda a: a[0].shape[1] if isinstance(a, tuple) else a.shape[1]
    T = rows(a_list[0])
    N = w.shape[2]
    tm, tn = _fit_tile(T, tm), _fit_tile(N, tn)
    n_head_tiles = 0
    args, specs = [], []
    for a in a_list:
        if isinstance(a, tuple):
            tm = _fit_tile(a[1].shape[0], _fit_tile(a[0].shape[0], tm))
    for a in a_list:
        if isinstance(a, tuple):
            nh = a[0].shape[0] // tm
            assert n_head_tiles in (0, nh)
            n_head_tiles = nh
            args += [a[0], a[1]]
            specs += [pl.BlockSpec((tm, width(a)), lambda i, j, nh=nh: (jnp.minimum(i, nh - 1), 0)),
                      pl.BlockSpec((tm, width(a)), lambda i, j, nh=nh: (jnp.maximum(i - nh, 0), 0))]
        else:
            args.append(a)
            specs.append(pl.BlockSpec((tm, width(a)), lambda i, j: (i, 0)))
    for k, a in enumerate(a_list):
        specs.append(pl.BlockSpec((None, width(a), tn), lambda i, j, k=k: (l, k, j)))
    return pl.pallas_call(
        functools.partial(_mm_kernel, split=tuple(split), n_head_tiles=n_head_tiles),
        out_shape=jax.ShapeDtypeStruct((T, N), F32),
        grid=(T // tm, N // tn), in_specs=specs,
        out_specs=pl.BlockSpec((tm, tn), lambda i, j: (i, j)),
        compiler_params=_cparams(("arbitrary", "arbitrary")), name="matmul",
    )(*args, *([w] * len(a_list)))


def _proj_in_kernel(h_ref, w_ref, z_ref, kt_ref, *, n_head_tiles, j0, nk):
    i, j = pl.program_id(0), pl.program_id(1)
    acc = jnp.dot(h_ref[...], w_ref[...].astype(BF16), preferred_element_type=F32)
    z_ref[...] = acc

    @pl.when(jnp.logical_and(jnp.logical_and(j >= j0, j < j0 + nk), i < n_head_tiles))
    def _():
        kt_ref[...] = acc.T


def proj_in_call(h, w, l, *, n_batch, seq, k_col0, k_width, tm=1024, tn=512):
    T, D = h.shape
    N = w.shape[2]
    n_head_rows = n_batch * seq
    tm = _fit_tile(seq, _fit_tile(T - n_head_rows, _fit_tile(T, tm)))
    tn = _fit_tile(k_width, _fit_tile(k_col0, _fit_tile(N, tn)))
    n_tiles, n_head, tpb = T // tm, n_head_rows // tm, seq // tm
    j0, nk = k_col0 // tn, k_width // tn
    row = lambda i: i

    def kt_index(i, j):
        head = i < n_head
        pt = jnp.where(head, i, n_head - 1)
        return (pt // tpb, jnp.where(head, jnp.clip(j - j0, 0, nk - 1), nk - 1), pt % tpb)

    return pl.pallas_call(
        functools.partial(_proj_in_kernel, n_head_tiles=n_head, j0=j0, nk=nk),
        out_shape=[jax.ShapeDtypeStruct((T, N), F32),
                   jax.ShapeDtypeStruct((n_batch, k_width, seq), F32)],
        grid=(n_tiles, N // tn),
        in_specs=[pl.BlockSpec((tm, D), lambda i, j: (row(i), 0)),
                  pl.BlockSpec((None, D, tn), lambda i, j: (l, 0, j))],
        out_specs=[pl.BlockSpec((tm, tn), lambda i, j: (row(i), j)),
                   pl.BlockSpec((None, tn, tm), kt_index)],
        compiler_params=_cparams(("arbitrary", "arbitrary")), name="proj_in",
    )(h, w)


def _lam_value(lamv_ref, lam_init):
    v = lamv_ref[...]
    s1 = jnp.sum(v[0:1] * v[1:2], axis=-1, keepdims=True)
    s2 = jnp.sum(v[2:3] * v[3:4], axis=-1, keepdims=True)
    return jnp.exp(s1) - jnp.exp(s2) + lam_init


def _alibi_slope(head):
    return lax.bitcast_convert_type((126 - head) << 23, F32)


def _attn_p_kernel(qi_tab, ki_tab, lamv_ref, q_ref, k_ref, v_ref, g_ref, o_ref,
                   q_s, m_s, l_s, acc_s, *, tq, lam_init):
    h = pl.program_id(1)
    p = pl.program_id(2)
    qi, ki = qi_tab[p], ki_tab[p]

    @pl.when(ki == 0)
    def _():
        q = q_ref[...] * (DA_HEAD_DIM ** -0.5 * LOG2E)
        lane = lax.broadcasted_iota(I32, q.shape, 1)
        q_s[0] = jnp.where(lane < DA_HEAD_DIM, q, 0.0).astype(BF16)
        q_s[1] = jnp.where(lane >= DA_HEAD_DIM, q, 0.0).astype(BF16)
        m_s[...] = jnp.full(m_s.shape, NEG, F32)
        l_s[...] = jnp.zeros(l_s.shape, F32)
        acc_s[...] = jnp.zeros(acc_s.shape, F32)

    def update(masked):
        kb, qb = min(KEY_PIECE, tq), min(QUERY_PIECE, tq)
        k = k_ref[...].astype(BF16)
        vt = v_ref[...].T.astype(BF16)
        slope = _alibi_slope(jnp.zeros((kb, qb), I32) + h) * LOG2E
        krows = [lax.broadcasted_iota(I32, (kb, qb), 0) + k0 for k0 in range(0, tq, kb)]
        biases = [slope * ((ki - qi) * tq + kr).astype(F32) for kr in krows]
        for q0 in range(0, tq, qb):
            cols = slice(q0, q0 + qb)
            qcol = lax.broadcasted_iota(I32, (kb, qb), 1) + q0
            for mi in range(2):
                m, l, acc = m_s[mi, :, cols], l_s[mi, :, cols], acc_s[mi, :, cols]
                for kp, k0 in enumerate(range(0, tq, kb)):
                    if masked and k0 > q0 + qb - 1:
                        continue
                    b = biases[kp]
                    if masked and k0 + kb - 1 > q0:
                        b = jnp.where(krows[kp] <= qcol, b, NEG)
                    s = lax.dot_general(k[k0:k0 + kb], q_s[mi, cols, :], NT_DIMS,
                                        preferred_element_type=F32) + b
                    m_new = jnp.maximum(m, jnp.max(s, axis=0, keepdims=True))
                    alpha = jnp.exp2(m - m_new)
                    pm = jnp.exp2(s - m_new)
                    l = alpha * l + jnp.sum(pm, axis=0, keepdims=True)
                    acc = alpha * acc + jnp.dot(vt[:, k0:k0 + kb], pm.astype(BF16),
                                                preferred_element_type=F32)
                    m = m_new
                m_s[mi, :, cols], l_s[mi, :, cols], acc_s[mi, :, cols] = m, l, acc

    @pl.when(ki < qi)
    def _():
        update(False)

    @pl.when(ki == qi)
    def _():
        update(True)
        lam = _lam_value(lamv_ref, lam_init)
        o = acc_s[0] / l_s[0] - lam * (acc_s[1] / l_s[1])
        ms = jnp.mean(o * o, axis=0, keepdims=True)
        on = o * lax.rsqrt(ms + EPS) * g_ref[...] * (1.0 - lam_init)
        o_ref[...] = on.T


def attn_prompt_call(z, lamv, g_subln_col, l, *, n_batch, seq, lam_init, tq=512):
    tq = min(tq, seq)
    nq = seq // tq
    pairs = [(qi, ki) for qi in range(nq) for ki in range(qi + 1)]
    qi_tab = jnp.asarray([p[0] for p in pairs], I32)
    ki_tab = jnp.asarray([p[1] for p in pairs], I32)
    H = DA_HEADS
    w = 2 * DA_HEAD_DIM
    grid_spec = pltpu.PrefetchScalarGridSpec(
        num_scalar_prefetch=2,
        grid=(n_batch, H, len(pairs)),
        in_specs=[
            pl.BlockSpec((None, 4, DA_HEAD_DIM), lambda b, h, p, qt, kt: (l, 0, 0)),
            pl.BlockSpec((tq, w), lambda b, h, p, qt, kt: (b * nq + qt[p], h)),
            pl.BlockSpec((tq, w), lambda b, h, p, qt, kt: (b * nq + kt[p], H + h)),
            pl.BlockSpec((tq, DA_VDIM), lambda b, h, p, qt, kt: (b * nq + kt[p], 2 * H + h)),
            pl.BlockSpec((None, DA_VDIM, 1), lambda b, h, p, qt, kt: (l, 0, 0)),
        ],
        out_specs=pl.BlockSpec((tq, DA_VDIM), lambda b, h, p, qt, kt: (b * nq + qt[p], h)),
        scratch_shapes=[
            pltpu.VMEM((2, tq, w), BF16),
            pltpu.VMEM((2, 1, tq), F32),
            pltpu.VMEM((2, 1, tq), F32),
            pltpu.VMEM((2, DA_VDIM, tq), F32),
        ],
    )
    return pl.pallas_call(
        functools.partial(_attn_p_kernel, tq=tq, lam_init=lam_init),
        out_shape=jax.ShapeDtypeStruct((n_batch * seq, H * DA_VDIM), F32),
        grid_spec=grid_spec,
        compiler_params=_cparams(("arbitrary", "arbitrary", "arbitrary")),
        name="attn_prompt",
    )(qi_tab, ki_tab, lamv, z, z, z, g_subln_col)


def _attn_s_kernel(pt_ref, lamv_ref, q_ref, kn_ref, vn_ref, g_ref, ckt_hbm, cv_hbm, o_ref,
                   kbuf, vbuf, sem, *, layer, n_dec, n_pages, dec_len, lam_init):
    H, W = DA_HEADS, 2 * DA_HEAD_DIM * DA_HEADS
    R = 2 * H * dec_len
    past = n_pages * PAGE_SIZE
    b = pl.program_id(0)
    slot = b % 2

    def start_pages(seq, s):
        for pg in range(n_pages):
            page = pt_ref[seq * n_pages + pg]
            pltpu.make_async_copy(ckt_hbm.at[layer, page], kbuf.at[s, pg], sem.at[s, 0]).start()
            pltpu.make_async_copy(cv_hbm.at[layer, page], vbuf.at[s, pg], sem.at[s, 1]).start()

    @pl.when(b == 0)
    def _():
        start_pages(0, 0)

    @pl.when(b + 1 < n_dec)
    def _():
        start_pages(b + 1, 1 - slot)

    pltpu.make_async_copy(ckt_hbm.at[layer, pl.ds(0, n_pages)], kbuf.at[slot], sem.at[slot, 0]).wait()
    pltpu.make_async_copy(cv_hbm.at[layer, pl.ds(0, n_pages)], vbuf.at[slot], sem.at[slot, 1]).wait()

    q = q_ref[...] * (DA_HEAD_DIM ** -0.5 * LOG2E)
    qt = jnp.broadcast_to(q[None], (2 * H, dec_len, W)).reshape(R, W)
    r = lax.broadcasted_iota(I32, (R, W), 0)
    c = lax.broadcasted_iota(I32, (R, W), 1)
    row_map, row_head = r // (H * dec_len), (r // dec_len) % H
    col_head, col_map = c // (2 * DA_HEAD_DIM), (c // DA_HEAD_DIM) % 2
    keep = jnp.logical_and(row_map == col_map, row_head == col_head)
    qx = jnp.where(keep, qt, 0.0).astype(BF16)
    lane = lax.broadcasted_iota(I32, (R, PAGE_SIZE), 1)
    rr = lax.broadcasted_iota(I32, (R, PAGE_SIZE), 0)
    slope = _alibi_slope((rr // dec_len) % H) * LOG2E
    tiles = []
    for pg in range(n_pages):
        kt = kbuf[slot, pg].astype(BF16)
        bias = slope * (lane + (pg * PAGE_SIZE - past)).astype(F32)
        tiles.append(jnp.dot(qx, kt, preferred_element_type=F32) + bias)
    zpad = jnp.zeros((PAGE_SIZE - dec_len, W), F32)
    kn = jnp.concatenate([kn_ref[...], zpad], axis=0).astype(BF16)
    s_new = lax.dot_general(qx, kn, NT_DIMS, preferred_element_type=F32) + slope * lane.astype(F32)
    tiles.append(jnp.where(lane <= rr % dec_len, s_new, NEG))

    m = tiles[0]
    for t in tiles[1:]:
        m = jnp.maximum(m, t)
    m = jnp.max(m, axis=-1, keepdims=True)
    ps = [jnp.exp2(t - m) for t in tiles]
    lsum = ps[0]
    for t in ps[1:]:
        lsum = lsum + t
    inv = 1.0 / jnp.sum(lsum, axis=-1, keepdims=True)

    lam = _lam_value(lamv_ref, lam_init)
    hq = H * dec_len
    for h in range(H):
        rows1 = slice(h * dec_len, (h + 1) * dec_len)
        rows2 = slice(hq + h * dec_len, hq + (h + 1) * dec_len)
        cols = slice(h * DA_VDIM, (h + 1) * DA_VDIM)
        ph = jnp.concatenate(
            [jnp.concatenate([t[rows1], t[rows2]], axis=0).astype(BF16) for t in ps], axis=1)
        vparts = [vbuf[slot, pg, pl.ds(h, PAGE_SIZE, stride=H), :].astype(BF16) for pg in range(n_pages)]
        vparts.append(jnp.concatenate([vn_ref[:, cols], zpad[:, :DA_VDIM]], axis=0).astype(BF16))
        acc = jnp.dot(ph, jnp.concatenate(vparts, axis=0), preferred_element_type=F32)
        o = acc[:dec_len] * inv[rows1] - lam * (acc[dec_len:] * inv[rows2])
        o_ref[:, cols] = _rms(o, g_ref[...]) * (1.0 - lam_init)


def attn_sample_call(z, cache_k, cache_v, page_table, lamv, g_subln, l, *, row0, dec_len, lam_init):
    n_dec, n_pages = page_table.shape
    H = DA_HEADS
    W = 2 * DA_HEAD_DIM * H
    n_layers, n_pool = cache_k.shape[:2]
    ckt = jnp.transpose(cache_k, (0, 1, 3, 4, 5, 2)).reshape(n_layers, n_pool, W, PAGE_SIZE)
    cv = cache_v.reshape(n_layers, n_pool, PAGE_SIZE * H, DA_VDIM)
    rb = row0 // dec_len
    grid_spec = pltpu.PrefetchScalarGridSpec(
        num_scalar_prefetch=1,
        grid=(n_dec,),
        in_specs=[
            pl.BlockSpec((None, 4, DA_HEAD_DIM), lambda b, pt: (l, 0, 0)),
            pl.BlockSpec((dec_len, W), lambda b, pt: (rb + b, 0)),
            pl.BlockSpec((dec_len, W), lambda b, pt: (rb + b, 1)),
            pl.BlockSpec((dec_len, W), lambda b, pt: (rb + b, 2)),
            pl.BlockSpec((None, 1, DA_VDIM), lambda b, pt: (l, 0, 0)),
            pl.BlockSpec(memory_space=pl.ANY),
            pl.BlockSpec(memory_space=pl.ANY),
        ],
        out_specs=pl.BlockSpec((dec_len, W), lambda b, pt: (b, 0)),
        scratch_shapes=[
            pltpu.VMEM((2, n_pages, W, PAGE_SIZE), F32),
            pltpu.VMEM((2, n_pages, PAGE_SIZE * H, DA_VDIM), F32),
            pltpu.SemaphoreType.DMA((2, 2)),
        ],
    )
    return pl.pallas_call(
        functools.partial(_attn_s_kernel, layer=l, n_dec=n_dec, n_pages=n_pages, dec_len=dec_len,
                          lam_init=lam_init),
        out_shape=jax.ShapeDtypeStruct((n_dec * dec_len, W), F32),
        grid_spec=grid_spec,
        compiler_params=_cparams(("arbitrary",)),
        name="attn_sample",
    )(page_table.reshape(-1), lamv, z, z, z, g_subln, ckt, cv)


def _gm_kernel(gu_ref, gv_ref, w_ref, b_ref, ng_ref, nb_ref, o_ref, gvo_ref, *, n_ptiles, dec_len):
    i = pl.program_id(0)
    r = lax.broadcasted_iota(I32, (CHUNK, CHUNK), 0)
    c = lax.broadcasted_iota(I32, (CHUNK, CHUNK), 1)
    same_seq = jnp.where(r // dec_len == c // dec_len, 1, 0) + jnp.where(i < n_ptiles, 1, 0)
    keep = jnp.logical_and(c <= r, same_seq > 0)
    for g in range(GM_HEADS):
        cols = slice(g * GM_DIM, (g + 1) * GM_DIM)
        v = jax.nn.gelu(gv_ref[:, cols])
        mu = jnp.mean(v, axis=-1, keepdims=True)
        var = jnp.mean(jnp.square(v - mu), axis=-1, keepdims=True)
        vn = (v - mu) * lax.rsqrt(var + EPS) * ng_ref[:, cols] + nb_ref[:, cols]
        gvo_ref[:, cols] = vn
        wm = jnp.where(keep, w_ref[g], 0.0).astype(BF16)
        mixed = jnp.dot(wm, vn.astype(BF16), preferred_element_type=F32) + b_ref[:, g:g + 1]
        o_ref[:, cols] = jax.nn.gelu(gu_ref[:, cols]) * mixed


def chunk_mlp_call(z, wmix, bmix, ng, nb, l, *, n_ptiles, dec_len, col0):
    T = z.shape[0]
    GW = GM_HEADS * GM_DIM
    cb = col0 // GW
    row = pl.BlockSpec((CHUNK, GW), lambda i: (i, 0))
    sel = lambda i: jnp.where(i < n_ptiles, 0, 1)
    return pl.pallas_call(
        functools.partial(_gm_kernel, n_ptiles=n_ptiles, dec_len=dec_len),
        out_shape=[jax.ShapeDtypeStruct((T, GW), F32)] * 2,
        grid=(T // CHUNK,),
        in_specs=[
            pl.BlockSpec((CHUNK, GW), lambda i: (i, cb)),
            pl.BlockSpec((CHUNK, GW), lambda i: (i, cb + 1)),
            pl.BlockSpec((None, GM_HEADS, CHUNK, CHUNK), lambda i: (sel(i), 0, 0, 0)),
            pl.BlockSpec((None, CHUNK, GM_HEADS), lambda i: (sel(i), 0, 0)),
            pl.BlockSpec((None, 1, GW), lambda i: (l, 0, 0)),
            pl.BlockSpec((None, 1, GW), lambda i: (l, 0, 0)),
        ],
        out_specs=[row, row],
        compiler_params=_cparams(("arbitrary",)), name="chunk_mlp",
    )(z, z, wmix, bmix, ng, nb)


def _ffn_kernel(te_ref, xi_ref, nv_ref, x_ref, wg_ref, wu_ref, wd_ref, o_ref, *, n_sub):
    i, f = pl.program_id(0), pl.program_id(1)
    tm = x_ref.shape[0]
    sub = tm // n_sub
    nv = nv_ref[i]

    @pl.when(f == 0)
    def _():
        o_ref[...] = jnp.zeros(o_ref.shape, F32)

    def swiglu_rows(rows):
        x = x_ref[rows, :]
        g = jnp.dot(x, wg_ref[...].astype(BF16), preferred_element_type=F32)
        u = jnp.dot(x, wu_ref[...].astype(BF16), preferred_element_type=F32)
        a = (g * jax.nn.sigmoid(g) * u).astype(BF16)
        o_ref[rows, :] += jnp.dot(a, wd_ref[...].astype(BF16), preferred_element_type=F32)

    n_need = (nv + (sub - 1)) // sub

    @pl.when(n_need == n_sub)
    def _():
        swiglu_rows(slice(None))

    for sb in range(n_sub - 1):
        @pl.when(jnp.logical_and(n_need < n_sub, sb < n_need))
        def _():
            swiglu_rows(slice(sb * sub, (sb + 1) * sub))


def ffn_call(x, w_gate, w_up, w_down, wl, tile_expert, tile_x, tile_valid, *, tm, tf=512, n_sub=4):
    P, D = x.shape
    F = w_gate.shape[-1]
    tf = min(tf, F)
    nf = F // tf
    n_tiles = P // tm

    def fidx(i, f, act):
        return jnp.where(act[i] > 0, f, nf - 1)

    grid_spec = pltpu.PrefetchScalarGridSpec(
        num_scalar_prefetch=3,
        grid=(n_tiles, nf),
        in_specs=[
            pl.BlockSpec((tm, D), lambda i, f, te, xi, act: (xi[i], 0), pipeline_mode=pl.Buffered(1)),
            pl.BlockSpec((None, None, D, tf), lambda i, f, te, xi, act: (wl, te[i], 0, fidx(i, f, act))),
            pl.BlockSpec((None, None, D, tf), lambda i, f, te, xi, act: (wl, te[i], 0, fidx(i, f, act))),
            pl.BlockSpec((None, None, tf, D), lambda i, f, te, xi, act: (wl, te[i], fidx(i, f, act), 0)),
        ],
        out_specs=pl.BlockSpec((tm, D), lambda i, f, te, xi, act: (i, 0), pipeline_mode=pl.Buffered(1)),
    )
    return pl.pallas_call(
        functools.partial(_ffn_kernel, n_sub=n_sub),
        out_shape=jax.ShapeDtypeStruct((P, D), F32),
        grid_spec=grid_spec,
        compiler_params=_cparams(("arbitrary", "arbitrary")), name="ffn",
    )(tile_expert, tile_x, tile_valid, x, w_gate, w_up, w_down)


def _split3(x):
    hi = x.astype(BF16)
    r1 = x - hi.astype(F32)
    mid = r1.astype(BF16)
    lo = (r1 - mid.astype(F32)).astype(BF16)
    return hi, mid, lo


def _router_kernel(h_ref, wr_ref, tri_ref, idx_ref, gate_ref, cnt_ref, carry_s, *, tm):
    i = pl.program_id(0)

    @pl.when(i == 0)
    def _():
        carry_s[...] = jnp.zeros(carry_s.shape, F32)

    hs = _split3(_rows_2d(h_ref))
    ws = _split3(wr_ref[...])
    logits = jnp.zeros((N_EXPERTS, tm), F32)
    for a in range(3):
        for b in range(3 - a):
            logits += lax.dot_general(ws[a], hs[b], NT_DIMS, preferred_element_type=F32)
    eid = lax.broadcasted_iota(I32, (N_EXPERTS, tm), 0).astype(F32)
    none = float(N_EXPERTS)
    m1 = jnp.max(logits, axis=0, keepdims=True)
    e1 = jnp.min(jnp.where(logits == m1, eid, none), axis=0, keepdims=True)
    is1 = eid == e1
    rest = jnp.where(is1, -jnp.inf, logits)
    m2 = jnp.max(rest, axis=0, keepdims=True)
    e2 = jnp.min(jnp.where(rest == m2, eid, none), axis=0, keepdims=True)
    is2 = eid == e2
    ex = jnp.exp(m2 - m1)
    g1 = 1.0 / (1.0 + ex)
    g2 = ex / (1.0 + ex)
    sel = jnp.where(jnp.logical_or(is1, is2), 1.0, 0.0)
    before = jnp.dot(sel.astype(BF16), tri_ref[...], preferred_element_type=F32) + carry_s[...]
    r1 = jnp.sum(jnp.where(is1, before, 0.0), axis=0, keepdims=True)
    r2 = jnp.sum(jnp.where(is2, before, 0.0), axis=0, keepdims=True)
    carry_s[...] = carry_s[...] + jnp.sum(sel, axis=1, keepdims=True)
    idx_ref[0:1, :] = e1.astype(I32)
    idx_ref[1:2, :] = e2.astype(I32)
    idx_ref[2:3, :] = r1.astype(I32)
    idx_ref[3:4, :] = r2.astype(I32)
    idx_ref[4:8, :] = jnp.zeros((4, tm), I32)
    gate_ref[0:1, :] = g1
    gate_ref[1:2, :] = g2
    gate_ref[2:8, :] = jnp.zeros((6, tm), F32)
    cnt_ref[...] = jnp.broadcast_to(carry_s[...], cnt_ref.shape).astype(I32)


def router_call(h, w_router, wl, *, tm=512):
    T, D = h.shape[0], h.shape[1] * h.shape[2]
    tm = _fit_tile(T, tm)
    n = T // tm
    wr_t = jnp.swapaxes(w_router, 1, 2)
    tri = (jnp.arange(tm)[:, None] < jnp.arange(tm)[None, :]).astype(BF16)
    idx, gate, cnt = pl.pallas_call(
        functools.partial(_router_kernel, tm=tm),
        out_shape=[jax.ShapeDtypeStruct((n, 8, tm), I32),
                   jax.ShapeDtypeStruct((n, 8, tm), F32),
                   jax.ShapeDtypeStruct((N_EXPERTS, 128), I32)],
        grid=(n,),
        in_specs=[
            pl.BlockSpec((tm,) + h.shape[1:], lambda i: (i, 0, 0)),
            pl.BlockSpec((None, N_EXPERTS, D), lambda i: (wl, 0, 0)),
            pl.BlockSpec((tm, tm), lambda i: (0, 0)),
        ],
        out_specs=[
            pl.BlockSpec((None, 8, tm), lambda i: (i, 0, 0)),
            pl.BlockSpec((None, 8, tm), lambda i: (i, 0, 0)),
            pl.BlockSpec((N_EXPERTS, 128), lambda i: (0, 0)),
        ],
        scratch_shapes=[pltpu.VMEM((N_EXPERTS, 1), F32)],
        compiler_params=_cparams(("arbitrary",)), name="router",
    )(h, wr_t, tri)
    rows = lambda a, k: a[:, k, :].reshape(T)
    return (rows(idx, 0), rows(idx, 1), rows(idx, 2), rows(idx, 3),
            rows(gate, 0), rows(gate, 1), cnt[:, 0])


def _start_row_copies(src_hbm, idx_ref, base, dst, sem, n):
    def body(r, carry):
        pltpu.make_async_copy(src_hbm.at[pl.ds(idx_ref[base + r], 1)],
                              dst.at[pl.ds(r, 1)], sem).start()
        return carry
    lax.fori_loop(0, n, body, 0, unroll=8)


def _wait_row_copies(src_hbm, dst, sem, n):
    pltpu.make_async_copy(src_hbm.at[pl.ds(0, n)], dst, sem).wait()


def _gather_kernel(src_ref, nact_ref, h_hbm, o_ref, buf, sem, *, tg):
    i = pl.program_id(0)
    nact = nact_ref[0]
    slot = i % 2

    def start(blk, s):
        _start_row_copies(h_hbm, src_ref, blk * tg, buf.at[s], sem.at[s], tg)

    @pl.when(jnp.logical_and(i == 0, nact > 0))
    def _():
        start(0, 0)

    @pl.when(i + 1 < nact)
    def _():
        start(i + 1, 1 - slot)

    @pl.when(i < nact)
    def _():
        _wait_row_copies(h_hbm, buf.at[slot], sem.at[slot], tg)
        o_ref[...] = _rows_2d(buf, slot).astype(o_ref.dtype)

    @pl.when(i >= nact)
    def _():
        o_ref[...] = jnp.zeros(o_ref.shape, o_ref.dtype)


def gather_call(h, src_token, n_active_blocks, *, n_slots, tg=256):
    D = h.shape[1] * h.shape[2]
    grid_spec = pltpu.PrefetchScalarGridSpec(
        num_scalar_prefetch=2,
        grid=(n_slots // tg,),
        in_specs=[pl.BlockSpec(memory_space=pl.ANY)],
        out_specs=pl.BlockSpec((tg, D), lambda i, src, nact: (i, 0)),
        scratch_shapes=[pltpu.VMEM((2, tg) + h.shape[1:], F32), pltpu.SemaphoreType.DMA((2,))],
    )
    return pl.pallas_call(
        functools.partial(_gather_kernel, tg=tg),
        out_shape=jax.ShapeDtypeStruct((n_slots, D), BF16),
        grid_spec=grid_spec,
        compiler_params=_cparams(("arbitrary",)), name="gather",
    )(src_token, n_active_blocks, h)


def _combine_kernel(s1_ref, s2_ref, y_hbm, g1_ref, g2_ref, x_ref, gpost_ref, gtp, gts, op_ref, os_ref,
                    buf, sem, *, tc, n_tiles, n_ptiles):
    i = pl.program_id(0)
    slot = i % 2

    def start(t, s):
        _start_row_copies(y_hbm, s1_ref, t * tc, buf.at[s, 0], sem.at[s, 0], tc)
        _start_row_copies(y_hbm, s2_ref, t * tc, buf.at[s, 1], sem.at[s, 1], tc)

    @pl.when(i == 0)
    def _():
        start(0, 0)

    @pl.when(i + 1 < n_tiles)
    def _():
        start(i + 1, 1 - slot)

    _wait_row_copies(y_hbm, buf.at[slot, 0], sem.at[slot, 0], tc)
    _wait_row_copies(y_hbm, buf.at[slot, 1], sem.at[slot, 1], tc)
    f = g1_ref[...] * buf[slot, 0] + g2_ref[...] * buf[slot, 1]
    out = x_ref[...] + _mod_value(i, n_ptiles, gtp, gts) * _rms(f, gpost_ref[...])

    @pl.when(i < n_ptiles)
    def _():
        op_ref[...] = out

    @pl.when(i >= n_ptiles)
    def _():
        os_ref[...] = out


def combine_call(geom, y, slot1, slot2, g1, g2, x, g_post, modp, mods, l, gt_col):
    T, D = x.shape
    tc = geom.tm
    npt = geom.n_ptiles
    row = pl.BlockSpec((tc, D), lambda i, *_: (i, 0))
    col = pl.BlockSpec((tc, 1), lambda i, *_: (i, 0))
    grid_spec = pltpu.PrefetchScalarGridSpec(
        num_scalar_prefetch=2,
        grid=(geom.n_tiles,),
        in_specs=[pl.BlockSpec(memory_space=pl.ANY), col, col, row,
                  pl.BlockSpec((None, 1, D), lambda i, *_: (l, 0, 0))]
                 + _mod_specs(geom, D, l, gt_col),
        out_specs=[pl.BlockSpec((tc, D), lambda i, *_: (jnp.minimum(i, npt - 1), 0)),
                   pl.BlockSpec((tc, D), lambda i, *_: (jnp.maximum(i - npt, 0), 0))],
        scratch_shapes=[pltpu.VMEM((2, 2, tc, D), F32), pltpu.SemaphoreType.DMA((2, 2))],
    )
    return pl.pallas_call(
        functools.partial(_combine_kernel, tc=tc, n_tiles=geom.n_tiles, n_ptiles=geom.n_ptiles),
        out_shape=[jax.ShapeDtypeStruct((npt * tc, D), F32),
                   jax.ShapeDtypeStruct((T - npt * tc, D), F32)],
        grid_spec=grid_spec,
        compiler_params=_cparams(("arbitrary",)), name="combine",
    )(slot1, slot2, y, g1.reshape(T, 1), g2.reshape(T, 1), x, g_post, modp, mods)


def moe_dispatch_plan(e1, e2, r1, r2, counts, *, tm, n_tiles_max):
    T = e1.shape[0]
    tiles_e = (counts + tm - 1) // tm
    tile_end = jnp.cumsum(tiles_e)
    tile_off = tile_end - tiles_e
    n_active = tile_end[-1]
    slot1 = tile_off[e1] * tm + r1
    slot2 = tile_off[e2] * tm + r2
    tok = jnp.arange(T, dtype=I32)
    src = jnp.zeros((n_tiles_max * tm,), I32).at[slot1].set(tok).at[slot2].set(tok)
    t = jnp.arange(n_tiles_max, dtype=I32)
    tile_x = jnp.minimum(t, n_active - 1).astype(I32)
    tile_expert = jnp.minimum(jnp.sum(tile_end[None, :] <= tile_x[:, None], axis=1), N_EXPERTS - 1).astype(I32)
    rows_left = counts[tile_expert] - (tile_x - tile_off[tile_expert]) * tm
    tile_valid = jnp.where(t < n_active, jnp.clip(rows_left, 0, tm), 0).astype(I32)
    return slot1.astype(I32), slot2.astype(I32), src, tile_expert, tile_x, tile_valid, n_active.astype(I32)


def kernel(x_prompt, x_sample, c_prompt, c_sample, cache_k, cache_v, page_table, g_pre_mix, g_post_mix, g_pre_ffn, g_post_ffn, w_ada, b_ada, w_in, w_out, lam_q1, lam_k1, lam_q2, lam_k2, g_subln, gm_ws, gm_b, gm_norm_g, gm_norm_b, w_d_gate, w_d_up, w_d_down, w_router, w_e_gate, w_e_up, w_e_down):
    B, S, D = x_prompt.shape
    Bd, Td, _ = x_sample.shape
    L = w_in.shape[0]
    NP, NS = B * S, Bd * Td
    T = NP + NS
    H = DA_HEADS
    QKW = 2 * DA_HEAD_DIM * H
    VW = DA_VDIM * H
    GW = GM_HEADS * GM_DIM
    assert S % CHUNK == 0 and NS % CHUNK == 0 and CHUNK % Td == 0

    tm_norm = min(256, NS)
    geom = RowGeom(B, S, Bd, Td, tm_norm)
    tm_ffn = min(1024, NS)
    assert T % tm_ffn == 0

    x = jnp.concatenate([x_prompt.reshape(NP, D), x_sample.reshape(NS, D)], axis=0)
    n_c = B + Bd
    c_all = jnp.concatenate([c_prompt, c_sample, jnp.zeros((-n_c % 8, D), F32)], axis=0)
    mod = ada_call(c_all, w_ada, b_ada)
    modp = mod[:, :B].reshape(L, B, 1, 6 * D)
    mods = jnp.repeat(mod[:, B:n_c], Td, axis=1)
    SH_M, SC_M, GT_M, SH_F, SC_F, GT_F = range(6)

    g3 = lambda a: a.reshape(L, 1, a.shape[-1])
    g_pre_mix3, g_post_mix3, g_pre_ffn3, g_post_ffn3 = map(g3, (g_pre_mix, g_post_mix, g_pre_ffn, g_post_ffn))
    g_subln3 = g3(g_subln)
    ng3 = gm_norm_g.reshape(L, 1, GW)
    nb3 = gm_norm_b.reshape(L, 1, GW)
    lamv = jnp.stack([lam_q1, lam_k1, lam_q2, lam_k2], axis=1)

    n_dense_tiles = T // tm_ffn
    dense_te = jnp.zeros((n_dense_tiles,), I32)
    dense_xi = jnp.arange(n_dense_tiles, dtype=I32)
    dense_act = jnp.full((n_dense_tiles,), tm_ffn, I32)

    (h,) = norm_call(geom, x, modp, mods, g_pre=g_pre_mix3, l_pre=0, sc_col=SC_M, sh_col=SH_M)

    k_p, v_p, k_s, v_s, gv_s = [], [], [], [], []
    for l in range(L):
        lam_init = 0.8 - 0.6 * math.exp(-0.3 * l)
        z, kt = proj_in_call(h, w_in, l, n_batch=B, seq=S, k_col0=QKW, k_width=QKW)
        k_p.append(jnp.transpose(kt.reshape(B, H, 2, DA_HEAD_DIM, S), (0, 4, 1, 2, 3)))
        v_p.append(z[:NP, 2 * QKW:2 * QKW + VW].reshape(B, S, H, DA_VDIM))
        k_s.append(z[NP:, QKW:2 * QKW].reshape(Bd, Td, H, 2, DA_HEAD_DIM))
        v_s.append(z[NP:, 2 * QKW:2 * QKW + VW].reshape(Bd, Td, H, DA_VDIM))

        o_att_p = attn_prompt_call(z, lamv, g_subln.reshape(L, DA_VDIM, 1), l, n_batch=B, seq=S,
                                   lam_init=lam_init)
        o_att_s = attn_sample_call(z, cache_k, cache_v, page_table, lamv, g_subln3, l,
                                   row0=NP, dec_len=Td, lam_init=lam_init)

        reps = CHUNK // Td
        wmix = jnp.stack([gm_ws[l], jnp.tile(gm_ws[l][:, :Td, :Td], (1, reps, reps))])
        bmix = jnp.stack([gm_b[l].T, jnp.tile(gm_b[l][:, :Td], (1, reps)).T])
        o_gm, gvn = chunk_mlp_call(z, wmix, bmix, ng3, nb3, l, n_ptiles=NP // CHUNK, dec_len=Td,
                                   col0=2 * QKW + VW)
        gv_s.append(gvn[NP:].reshape(Bd, Td, GM_HEADS, GM_DIM))

        m = mm_call([(o_att_p, o_att_s), o_gm], w_out, l)
        is_moe = l % 2 == 1
        x, h2 = norm_call(geom, x, modp, mods, m=m, g_post=g_post_mix3, l_post=l, gt_col=GT_M,
                          g_pre=g_pre_ffn3, l_pre=l, sc_col=SC_F, sh_col=SH_F,
                          h_dtype=F32 if is_moe else BF16, h_row_major=is_moe)
        wl = l // 2
        if not is_moe:
            f = ffn_call(h2, w_d_gate[:, None], w_d_up[:, None], w_d_down[:, None], wl,
                         dense_te, dense_xi, dense_act, tm=tm_ffn)
            if l + 1 < L:
                x, h = norm_call(geom, x, modp, mods, m=f, g_post=g_post_ffn3, l_post=l, gt_col=GT_F,
                                 g_pre=g_pre_mix3, l_pre=l + 1, sc_col=SC_M, sh_col=SH_M)
            else:
                (x,) = norm_call(geom, x, modp, mods, m=f, g_post=g_post_ffn3, l_post=l, gt_col=GT_F)
        else:
            e1, e2, r1, r2, g1, g2, counts = router_call(h2, w_router, wl)
            n_tiles_max = (2 * T + N_EXPERTS * (tm_ffn - 1)) // tm_ffn
            slot1, slot2, src, te, xi, act, n_active = moe_dispatch_plan(
                e1, e2, r1, r2, counts, tm=tm_ffn, n_tiles_max=n_tiles_max)
            tg = min(256, tm_ffn)
            xs = gather_call(h2, src, (n_active * (tm_ffn // tg)).reshape(1),
                             n_slots=n_tiles_max * tm_ffn, tg=tg)
            y = ffn_call(xs, w_e_gate, w_e_up, w_e_down, wl, te, xi, act, tm=tm_ffn)
            x_halves = combine_call(geom, y, slot1, slot2, g1, g2, x, g_post_ffn3, modp, mods, l, GT_F)
            x = None
            if l + 1 < L:
                x = jnp.concatenate(x_halves, axis=0)
                (h,) = norm_call(geom, x, modp, mods, g_pre=g_pre_mix3, l_pre=l + 1, sc_col=SC_M, sh_col=SH_M)

    if x is not None:
        x_halves = (x[:NP], x[NP:])
    y_prompt = x_halves[0].reshape(B, S, D)
    y_sample = x_halves[1].reshape(Bd, Td, D)
    return (y_prompt, y_sample, jnp.stack(k_p), jnp.stack(v_p), jnp.stack(k_s), jnp.stack(v_s),
            jnp.stack(gv_s))
```

```python
import functools
import math

import jax
import jax.numpy as jnp
from jax import lax
from jax.experimental import pallas as pl
from jax.experimental.pallas import tpu as pltpu

F32 = jnp.float32
BF16 = jnp.bfloat16
I32 = jnp.int32

EPS = 1e-6
DA_HEADS = 8
DA_HEAD_DIM = 64
DA_VDIM = 128
GM_HEADS = 8
GM_DIM = 128
CHUNK = 128
PAGE_SIZE = 128
N_EXPERTS = 8
NEG = -1e30
LOG2E = 1.4426950408889634

V7X_VMEM_BYTES = 64 * 1024 * 1024
VMEM_LIMIT = V7X_VMEM_BYTES - 8 * 1024 * 1024

NT_DIMS = (((1,), (1,)), ((), ()))

LANES = 128


def _rows_2d(ref3, *lead):
    return jnp.concatenate([ref3[(*lead, slice(None), j, slice(None))] for j in range(ref3.shape[-2])], axis=1)


KEY_PIECE = 256
QUERY_PIECE = 128


def _cparams(sem):
    return pltpu.CompilerParams(dimension_semantics=sem, vmem_limit_bytes=VMEM_LIMIT)


def _fit_tile(n, t):
    while n % t:
        t //= 2
    return t


def _rms(x, g):
    return x * lax.rsqrt(jnp.mean(x * x, axis=-1, keepdims=True) + EPS) * g


def _ada_kernel(c_ref, w_ref, b_ref, o_ref):
    c = c_ref[...]
    a = (c * jax.nn.sigmoid(c)).astype(BF16)
    o_ref[...] = jnp.dot(a, w_ref[...].astype(BF16), preferred_element_type=F32) + b_ref[...]


def ada_call(c_all, w_ada, b_ada):
    R, D = c_all.shape
    L, _, N = w_ada.shape
    tn = min(1024, N)
    return pl.pallas_call(
        _ada_kernel,
        out_shape=jax.ShapeDtypeStruct((L, R, N), F32),
        grid=(L, N // tn),
        in_specs=[
            pl.BlockSpec((R, D), lambda l, j: (0, 0)),
            pl.BlockSpec((None, D, tn), lambda l, j: (l, 0, j)),
            pl.BlockSpec((None, 1, tn), lambda l, j: (l, 0, j)),
        ],
        out_specs=pl.BlockSpec((None, R, tn), lambda l, j: (l, 0, j)),
        compiler_params=_cparams(("arbitrary", "arbitrary")),
        name="ada",
    )(c_all, w_ada, b_ada.reshape(L, 1, N))


class RowGeom:
    def __init__(self, n_batch, seq, n_dec, dec_len, tm):
        assert seq % tm == 0 and (n_dec * dec_len) % tm == 0
        self.tm = tm
        self.n_batch = n_batch
        self.tiles_per_batch = seq // tm
        self.n_ptiles = n_batch * self.tiles_per_batch
        self.n_tiles = self.n_ptiles + (n_dec * dec_len) // tm


def _mod_specs(geom, D, l, col):
    tpb, npt, nb = geom.tiles_per_batch, geom.n_ptiles, geom.n_batch
    sp = pl.BlockSpec((None, None, 1, D),
                      lambda i, *_: (l, jnp.minimum(i // tpb, nb - 1), 0, col))
    ss = pl.BlockSpec((None, geom.tm, D),
                      lambda i, *_: (l, jnp.maximum(i - npt, 0), col))
    return [sp, ss]


def _mod_value(i, n_ptiles, p_ref, s_ref):
    return jnp.where(i < n_ptiles, p_ref[...], s_ref[...])


def _norm_kernel(*refs, n_ptiles, post, pre):
    it = iter(refs)
    x_ref = next(it)
    if post:
        m_ref, gpost_ref, gtp, gts = next(it), next(it), next(it), next(it)
    if pre:
        gpre_ref, scp, scs, shp, shs = next(it), next(it), next(it), next(it), next(it)
    if post:
        xo_ref = next(it)
    if pre:
        h_ref = next(it)
    i = pl.program_id(0)
    x = x_ref[...]
    if post:
        x = x + _mod_value(i, n_ptiles, gtp, gts) * _rms(m_ref[...], gpost_ref[...])
        xo_ref[...] = x
    if pre:
        h = _rms(x, gpre_ref[...]) * (1.0 + _mod_value(i, n_ptiles, scp, scs)) \
            + _mod_value(i, n_ptiles, shp, shs)
        h_ref[...] = h.astype(h_ref.dtype).reshape(h_ref.shape)


def norm_call(geom, x, modp, mods, *, m=None, g_post=None, l_post=None, gt_col=None,
              g_pre=None, l_pre=None, sc_col=None, sh_col=None, h_dtype=BF16, h_row_major=False):
    T, D = x.shape
    tm = geom.tm
    post, pre = m is not None, g_pre is not None
    row = pl.BlockSpec((tm, D), lambda i: (i, 0))
    args, specs = [x], [row]
    if post:
        args += [m, g_post, modp, mods]
        specs += [row, pl.BlockSpec((None, 1, D), lambda i: (l_post, 0, 0))]
        specs += _mod_specs(geom, D, l_post, gt_col)
    if pre:
        args += [g_pre, modp, mods, modp, mods]
        specs += [pl.BlockSpec((None, 1, D), lambda i: (l_pre, 0, 0))]
        specs += _mod_specs(geom, D, l_pre, sc_col) + _mod_specs(geom, D, l_pre, sh_col)
    out_shape, out_specs = [], []
    if post:
        out_shape.append(jax.ShapeDtypeStruct((T, D), F32))
        out_specs.append(row)
    if pre and h_row_major:
        out_shape.append(jax.ShapeDtypeStruct((T, D // LANES, LANES), h_dtype))
        out_specs.append(pl.BlockSpec((tm, D // LANES, LANES), lambda i: (i, 0, 0)))
    elif pre:
        out_shape.append(jax.ShapeDtypeStruct((T, D), h_dtype))
        out_specs.append(row)
    outs = pl.pallas_call(
        functools.partial(_norm_kernel, n_ptiles=geom.n_ptiles, post=post, pre=pre),
        out_shape=out_shape, grid=(geom.n_tiles,), in_specs=specs, out_specs=out_specs,
        compiler_params=_cparams(("arbitrary",)), name="norm",
    )(*args)
    return outs


def _mm_kernel(*refs, split, n_head_tiles):
    it = iter(refs)
    a_refs = [(next(it), next(it)) if s else next(it) for s in split]
    w_refs = [next(it) for _ in split]
    o_ref = next(it)
    i = pl.program_id(0)
    acc = None
    for a_ref, w_ref in zip(a_refs, w_refs):
        a = jnp.where(i < n_head_tiles, a_ref[0][...], a_ref[1][...]) if isinstance(a_ref, tuple) else a_ref[...]
        d = jnp.dot(a.astype(BF16), w_ref[...].astype(BF16), preferred_element_type=F32)
        acc = d if acc is None else acc + d
    o_ref[...] = acc


def mm_call(a_list, w, l, *, tm=1024, tn=512):
    split = [isinstance(a, tuple) for a in a_list]
    rows = lambda a: a[0].shape[0] + a[1].shape[0] if isinstance(a, tuple) else a.shape[0]
    width = lambda a: a[0].shape[1] if isinstance(a, tuple) else a.shape[1]
    T = rows(a_list[0])
    N = w.shape[2]
    tm, tn = _fit_tile(T, tm), _fit_tile(N, tn)
    n_head_tiles = 0
    args, specs = [], []
    for a in a_list:
        if isinstance(a, tuple):
            tm = _fit_tile(a[1].shape[0], _fit_tile(a[0].shape[0], tm))
    for a in a_list:
        if isinstance(a, tuple):
            nh = a[0].shape[0] // tm
            assert n_head_tiles in (0, nh)
            n_head_tiles = nh
            args += [a[0], a[1]]
            specs += [pl.BlockSpec((tm, width(a)), lambda i, j, nh=nh: (jnp.minimum(i, nh - 1), 0)),
                      pl.BlockSpec((tm, width(a)), lambda i, j, nh=nh: (jnp.maximum(i - nh, 0), 0))]
        else:
            args.append(a)
            specs.append(pl.BlockSpec((tm, width(a)), lambda i, j: (i, 0)))
    for k, a in enumerate(a_list):
        specs.append(pl.BlockSpec((None, width(a), tn), lambda i, j, k=k: (l, k, j)))
    return pl.pallas_call(
        functools.partial(_mm_kernel, split=tuple(split), n_head_tiles=n_head_tiles),
        out_shape=jax.ShapeDtypeStruct((T, N), F32),
        grid=(T // tm, N // tn), in_specs=specs,
        out_specs=pl.BlockSpec((tm, tn), lambda i, j: (i, j)),
        compiler_params=_cparams(("arbitrary", "arbitrary")), name="matmul",
    )(*args, *([w] * len(a_list)))


def _proj_in_kernel(h_ref, w_ref, z_ref, kt_ref, *, n_head_tiles, j0, nk):
    i, j = pl.program_id(0), pl.program_id(1)
    acc = jnp.dot(h_ref[...], w_ref[...].astype(BF16), preferred_element_type=F32)
    z_ref[...] = acc

    @pl.when(jnp.logical_and(jnp.logical_and(j >= j0, j < j0 + nk), i < n_head_tiles))
    def _():
        kt_ref[...] = acc.T


def proj_in_call(h, w, l, *, n_batch, seq, k_col0, k_width, tm=1024, tn=512):
    T, D = h.shape
    N = w.shape[2]
    n_head_rows = n_batch * seq
    tm = _fit_tile(seq, _fit_tile(T - n_head_rows, _fit_tile(T, tm)))
    tn = _fit_tile(k_width, _fit_tile(k_col0, _fit_tile(N, tn)))
    n_tiles, n_head, tpb = T // tm, n_head_rows // tm, seq // tm
    j0, nk = k_col0 // tn, k_width // tn
    row = lambda i: i

    def kt_index(i, j):
        head = i < n_head
        pt = jnp.where(head, i, n_head - 1)
        return (pt // tpb, jnp.where(head, jnp.clip(j - j0, 0, nk - 1), nk - 1), pt % tpb)

    return pl.pallas_call(
        functools.partial(_proj_in_kernel, n_head_tiles=n_head, j0=j0, nk=nk),
        out_shape=[jax.ShapeDtypeStruct((T, N), F32),
                   jax.ShapeDtypeStruct((n_batch, k_width, seq), F32)],
        grid=(n_tiles, N // tn),
        in_specs=[pl.BlockSpec((tm, D), lambda i, j: (row(i), 0)),
                  pl.BlockSpec((None, D, tn), lambda i, j: (l, 0, j))],
        out_specs=[pl.BlockSpec((tm, tn), lambda i, j: (row(i), j)),
                   pl.BlockSpec((None, tn, tm), kt_index)],
        compiler_params=_cparams(("arbitrary", "arbitrary")), name="proj_in",
    )(h, w)


def _lam_value(lamv_ref, lam_init):
    v = lamv_ref[...]
    s1 = jnp.sum(v[0:1] * v[1:2], axis=-1, keepdims=True)
    s2 = jnp.sum(v[2:3] * v[3:4], axis=-1, keepdims=True)
    return jnp.exp(s1) - jnp.exp(s2) + lam_init


def _alibi_slope(head):
    return lax.bitcast_convert_type((126 - head) << 23, F32)


def _attn_p_kernel(qi_tab, ki_tab, lamv_ref, q_ref, k_ref, v_ref, g_ref, o_ref,
                   q_s, m_s, l_s, acc_s, *, tq, lam_init):
    h = pl.program_id(1)
    p = pl.program_id(2)
    qi, ki = qi_tab[p], ki_tab[p]

    @pl.when(ki == 0)
    def _():
        q = q_ref[...] * (DA_HEAD_DIM ** -0.5 * LOG2E)
        lane = lax.broadcasted_iota(I32, q.shape, 1)
        q_s[0] = jnp.where(lane < DA_HEAD_DIM, q, 0.0).astype(BF16)
        q_s[1] = jnp.where(lane >= DA_HEAD_DIM, q, 0.0).astype(BF16)
        m_s[...] = jnp.full(m_s.shape, NEG, F32)
        l_s[...] = jnp.zeros(l_s.shape, F32)
        acc_s[...] = jnp.zeros(acc_s.shape, F32)

    def update(masked):
        kb, qb = min(KEY_PIECE, tq), min(QUERY_PIECE, tq)
        k = k_ref[...].astype(BF16)
        vt = v_ref[...].T.astype(BF16)
        slope = _alibi_slope(jnp.zeros((kb, qb), I32) + h) * LOG2E
        krows = [lax.broadcasted_iota(I32, (kb, qb), 0) + k0 for k0 in range(0, tq, kb)]
        biases = [slope * ((ki - qi) * tq + kr).astype(F32) for kr in krows]
        for q0 in range(0, tq, qb):
            cols = slice(q0, q0 + qb)
            qcol = lax.broadcasted_iota(I32, (kb, qb), 1) + q0
            for mi in range(2):
                m, l, acc = m_s[mi, :, cols], l_s[mi, :, cols], acc_s[mi, :, cols]
                for kp, k0 in enumerate(range(0, tq, kb)):
                    if masked and k0 > q0 + qb - 1:
                        continue
                    b = biases[kp]
                    if masked and k0 + kb - 1 > q0:
                        b = jnp.where(krows[kp] <= qcol, b, NEG)
                    s = lax.dot_general(k[k0:k0 + kb], q_s[mi, cols, :], NT_DIMS,
                                        preferred_element_type=F32) + b
                    m_new = jnp.maximum(m, jnp.max(s, axis=0, keepdims=True))
                    alpha = jnp.exp2(m - m_new)
                    pm = jnp.exp2(s - m_new)
                    l = alpha * l + jnp.sum(pm, axis=0, keepdims=True)
                    acc = alpha * acc + jnp.dot(vt[:, k0:k0 + kb], pm.astype(BF16),
                                                preferred_element_type=F32)
                    m = m_new
                m_s[mi, :, cols], l_s[mi, :, cols], acc_s[mi, :, cols] = m, l, acc

    @pl.when(ki < qi)
    def _():
        update(False)

    @pl.when(ki == qi)
    def _():
        update(True)
        lam = _lam_value(lamv_ref, lam_init)
        o = acc_s[0] / l_s[0] - lam * (acc_s[1] / l_s[1])
        ms = jnp.mean(o * o, axis=0, keepdims=True)
        on = o * lax.rsqrt(ms + EPS) * g_ref[...] * (1.0 - lam_init)
        o_ref[...] = on.T


def _attn_s_body(b, pt_ref, lamv_ref, q_ref, kn_ref, vn_ref, g_ref, ckt_hbm, cv_hbm, o_ref,
                 kbuf, vbuf, sem, *, layer, n_dec, n_pages, dec_len, lam_init):
    H, W = DA_HEADS, 2 * DA_HEAD_DIM * DA_HEADS
    R = 2 * H * dec_len
    past = n_pages * PAGE_SIZE
    slot = b % 2

    def start_pages(seq, s):
        for pg in range(n_pages):
            page = pt_ref[seq * n_pages + pg]
            pltpu.make_async_copy(ckt_hbm.at[layer, page], kbuf.at[s, pg], sem.at[s, 0]).start()
            pltpu.make_async_copy(cv_hbm.at[layer, page], vbuf.at[s, pg], sem.at[s, 1]).start()

    @pl.when(b == 0)
    def _():
        start_pages(0, 0)

    @pl.when(b + 1 < n_dec)
    def _():
        start_pages(b + 1, 1 - slot)

    pltpu.make_async_copy(ckt_hbm.at[layer, pl.ds(0, n_pages)], kbuf.at[slot], sem.at[slot, 0]).wait()
    pltpu.make_async_copy(cv_hbm.at[layer, pl.ds(0, n_pages)], vbuf.at[slot], sem.at[slot, 1]).wait()

    q = q_ref[...] * (DA_HEAD_DIM ** -0.5 * LOG2E)
    qt = jnp.broadcast_to(q[None], (2 * H, dec_len, W)).reshape(R, W)
    r = lax.broadcasted_iota(I32, (R, W), 0)
    c = lax.broadcasted_iota(I32, (R, W), 1)
    row_map, row_head = r // (H * dec_len), (r // dec_len) % H
    col_head, col_map = c // (2 * DA_HEAD_DIM), (c // DA_HEAD_DIM) % 2
    keep = jnp.logical_and(row_map == col_map, row_head == col_head)
    qx = jnp.where(keep, qt, 0.0).astype(BF16)
    lane = lax.broadcasted_iota(I32, (R, PAGE_SIZE), 1)
    rr = lax.broadcasted_iota(I32, (R, PAGE_SIZE), 0)
    slope = _alibi_slope((rr // dec_len) % H) * LOG2E
    tiles = []
    for pg in range(n_pages):
        kt = kbuf[slot, pg].astype(BF16)
        bias = slope * (lane + (pg * PAGE_SIZE - past)).astype(F32)
        tiles.append(jnp.dot(qx, kt, preferred_element_type=F32) + bias)
    zpad = jnp.zeros((PAGE_SIZE - dec_len, W), F32)
    kn = jnp.concatenate([kn_ref[...], zpad], axis=0).astype(BF16)
    s_new = lax.dot_general(qx, kn, NT_DIMS, preferred_element_type=F32) + slope * lane.astype(F32)
    tiles.append(jnp.where(lane <= rr % dec_len, s_new, NEG))

    m = tiles[0]
    for t in tiles[1:]:
        m = jnp.maximum(m, t)
    m = jnp.max(m, axis=-1, keepdims=True)
    ps = [jnp.exp2(t - m) for t in tiles]
    lsum = ps[0]
    for t in ps[1:]:
        lsum = lsum + t
    inv = 1.0 / jnp.sum(lsum, axis=-1, keepdims=True)

    lam = _lam_value(lamv_ref, lam_init)
    hq = H * dec_len
    for h in range(H):
        rows1 = slice(h * dec_len, (h + 1) * dec_len)
        rows2 = slice(hq + h * dec_len, hq + (h + 1) * dec_len)
        cols = slice(h * DA_VDIM, (h + 1) * DA_VDIM)
        ph = jnp.concatenate(
            [jnp.concatenate([t[rows1], t[rows2]], axis=0).astype(BF16) for t in ps], axis=1)
        vparts = [vbuf[slot, pg, pl.ds(h, PAGE_SIZE, stride=H), :].astype(BF16) for pg in range(n_pages)]
        vparts.append(jnp.concatenate([vn_ref[:, cols], zpad[:, :DA_VDIM]], axis=0).astype(BF16))
        acc = jnp.dot(ph, jnp.concatenate(vparts, axis=0), preferred_element_type=F32)
        o = acc[:dec_len] * inv[rows1] - lam * (acc[dec_len:] * inv[rows2])
        o_ref[:, cols] = _rms(o, g_ref[...]) * (1.0 - lam_init)


def _attn_kernel(qi_tab, ki_tab, pt_ref, lamv_ref, q_ref, k_ref, v_ref, gcol_ref,
                 sq_ref, skn_ref, svn_ref, grow_ref, ckt_hbm, cv_hbm, op_ref, os_ref,
                 q_s, m_s, l_s, acc_s, kbuf, vbuf, sem, *, tq, n_pairs, period, sample_kw):
    _attn_p_kernel(qi_tab, ki_tab, lamv_ref, q_ref, k_ref, v_ref, gcol_ref, op_ref,
                   q_s, m_s, l_s, acc_s, tq=tq, lam_init=sample_kw["lam_init"])
    step = (pl.program_id(0) * DA_HEADS + pl.program_id(1)) * n_pairs + pl.program_id(2)
    seq = step // period

    @pl.when(jnp.logical_and(step % period == 0, seq < sample_kw["n_dec"]))
    def _():
        _attn_s_body(seq, pt_ref, lamv_ref, sq_ref, skn_ref, svn_ref, grow_ref, ckt_hbm, cv_hbm, os_ref,
                     kbuf, vbuf, sem, **sample_kw)


def attn_call(z, cache_k, cache_v, page_table, lamv, g_subln, l, *, n_batch, seq, dec_len, lam_init, tq=512):
    n_dec, n_pages = page_table.shape
    H = DA_HEADS
    w = 2 * DA_HEAD_DIM
    W = w * H
    n_layers, n_pool = cache_k.shape[:2]
    ckt = jnp.transpose(cache_k, (0, 1, 3, 4, 5, 2)).reshape(n_layers, n_pool, W, PAGE_SIZE)
    cv = cache_v.reshape(n_layers, n_pool, PAGE_SIZE * H, DA_VDIM)
    tq = min(tq, seq)
    nq = seq // tq
    pairs = [(qi, ki) for qi in range(nq) for ki in range(qi + 1)]
    qi_tab = jnp.asarray([p[0] for p in pairs], I32)
    ki_tab = jnp.asarray([p[1] for p in pairs], I32)
    n_pairs = len(pairs)
    period = (n_batch * H * n_pairs) // n_dec
    assert period >= 1, "fewer prompt steps than sample sequences"
    rb = (n_batch * seq) // dec_len

    def sample_seq(b, h, p):
        return jnp.minimum(((b * H + h) * n_pairs + p) // period, n_dec - 1)

    grid_spec = pltpu.PrefetchScalarGridSpec(
        num_scalar_prefetch=3,
        grid=(n_batch, H, n_pairs),
        in_specs=[
            pl.BlockSpec((None, 4, DA_HEAD_DIM), lambda b, h, p, qt, kt, pt: (l, 0, 0)),
            pl.BlockSpec((tq, w), lambda b, h, p, qt, kt, pt: (b * nq + qt[p], h)),
            pl.BlockSpec((tq, w), lambda b, h, p, qt, kt, pt: (b * nq + kt[p], H + h)),
            pl.BlockSpec((tq, DA_VDIM), lambda b, h, p, qt, kt, pt: (b * nq + kt[p], 2 * H + h)),
            pl.BlockSpec((None, DA_VDIM, 1), lambda b, h, p, qt, kt, pt: (l, 0, 0)),
            pl.BlockSpec((dec_len, W), lambda b, h, p, qt, kt, pt: (rb + sample_seq(b, h, p), 0)),
            pl.BlockSpec((dec_len, W), lambda b, h, p, qt, kt, pt: (rb + sample_seq(b, h, p), 1)),
            pl.BlockSpec((dec_len, W), lambda b, h, p, qt, kt, pt: (rb + sample_seq(b, h, p), 2)),
            pl.BlockSpec((None, 1, DA_VDIM), lambda b, h, p, qt, kt, pt: (l, 0, 0)),
            pl.BlockSpec(memory_space=pl.ANY),
            pl.BlockSpec(memory_space=pl.ANY),
        ],
        out_specs=[
            pl.BlockSpec((tq, DA_VDIM), lambda b, h, p, qt, kt, pt: (b * nq + qt[p], h)),
            pl.BlockSpec((dec_len, W), lambda b, h, p, qt, kt, pt: (sample_seq(b, h, p), 0)),
        ],
        scratch_shapes=[
            pltpu.VMEM((2, tq, w), BF16),
            pltpu.VMEM((2, 1, tq), F32),
            pltpu.VMEM((2, 1, tq), F32),
            pltpu.VMEM((2, DA_VDIM, tq), F32),
            pltpu.VMEM((2, n_pages, W, PAGE_SIZE), F32),
            pltpu.VMEM((2, n_pages, PAGE_SIZE * H, DA_VDIM), F32),
            pltpu.SemaphoreType.DMA((2, 2)),
        ],
    )
    sample_kw = dict(layer=l, n_dec=n_dec, n_pages=n_pages, dec_len=dec_len, lam_init=lam_init)
    return pl.pallas_call(
        functools.partial(_attn_kernel, tq=tq, n_pairs=n_pairs, period=period, sample_kw=sample_kw),
        out_shape=[jax.ShapeDtypeStruct((n_batch * seq, W), F32),
                   jax.ShapeDtypeStruct((n_dec * dec_len, W), F32)],
        grid_spec=grid_spec,
        compiler_params=_cparams(("arbitrary", "arbitrary", "arbitrary")),
        name="attn",
    )(qi_tab, ki_tab, page_table.reshape(-1), lamv, z, z, z, g_subln.reshape(-1, DA_VDIM, 1),
      z, z, z, g_subln.reshape(-1, 1, DA_VDIM), ckt, cv)


def _gm_kernel(gu_ref, gv_ref, w_ref, b_ref, ng_ref, nb_ref, o_ref, gvo_ref, *, n_ptiles, dec_len):
    i = pl.program_id(0)
    r = lax.broadcasted_iota(I32, (CHUNK, CHUNK), 0)
    c = lax.broadcasted_iota(I32, (CHUNK, CHUNK), 1)
    same_seq = jnp.where(r // dec_len == c // dec_len, 1, 0) + jnp.where(i < n_ptiles, 1, 0)
    keep = jnp.logical_and(c <= r, same_seq > 0)
    for g in range(GM_HEADS):
        cols = slice(g * GM_DIM, (g + 1) * GM_DIM)
        v = jax.nn.gelu(gv_ref[:, cols])
        mu = jnp.mean(v, axis=-1, keepdims=True)
        var = jnp.mean(jnp.square(v - mu), axis=-1, keepdims=True)
        vn = (v - mu) * lax.rsqrt(var + EPS) * ng_ref[:, cols] + nb_ref[:, cols]
        gvo_ref[:, cols] = vn
        wm = jnp.where(keep, w_ref[g], 0.0).astype(BF16)
        mixed = jnp.dot(wm, vn.astype(BF16), preferred_element_type=F32) + b_ref[:, g:g + 1]
        o_ref[:, cols] = jax.nn.gelu(gu_ref[:, cols]) * mixed


def chunk_mlp_call(z, wmix, bmix, ng, nb, l, *, n_ptiles, dec_len, col0):
    T = z.shape[0]
    GW = GM_HEADS * GM_DIM
    cb = col0 // GW
    row = pl.BlockSpec((CHUNK, GW), lambda i: (i, 0))
    sel = lambda i: jnp.where(i < n_ptiles, 0, 1)
    return pl.pallas_call(
        functools.partial(_gm_kernel, n_ptiles=n_ptiles, dec_len=dec_len),
        out_shape=[jax.ShapeDtypeStruct((T, GW), F32)] * 2,
        grid=(T // CHUNK,),
        in_specs=[
            pl.BlockSpec((CHUNK, GW), lambda i: (i, cb)),
            pl.BlockSpec((CHUNK, GW), lambda i: (i, cb + 1)),
            pl.BlockSpec((None, GM_HEADS, CHUNK, CHUNK), lambda i: (sel(i), 0, 0, 0)),
            pl.BlockSpec((None, CHUNK, GM_HEADS), lambda i: (sel(i), 0, 0)),
            pl.BlockSpec((None, 1, GW), lambda i: (l, 0, 0)),
            pl.BlockSpec((None, 1, GW), lambda i: (l, 0, 0)),
        ],
        out_specs=[row, row],
        compiler_params=_cparams(("arbitrary",)), name="chunk_mlp",
    )(z, z, wmix, bmix, ng, nb)


def _ffn_kernel(te_ref, xi_ref, nv_ref, x_ref, wg_ref, wu_ref, wd_ref, o_ref, *, n_sub):
    i, f = pl.program_id(0), pl.program_id(1)
    tm = x_ref.shape[0]
    sub = tm // n_sub
    nv = nv_ref[i]

    @pl.when(f == 0)
    def _():
        o_ref[...] = jnp.zeros(o_ref.shape, F32)

    def swiglu_rows(rows):
        x = x_ref[rows, :]
        g = jnp.dot(x, wg_ref[...].astype(BF16), preferred_element_type=F32)
        u = jnp.dot(x, wu_ref[...].astype(BF16), preferred_element_type=F32)
        a = (g * jax.nn.sigmoid(g) * u).astype(BF16)
        o_ref[rows, :] += jnp.dot(a, wd_ref[...].astype(BF16), preferred_element_type=F32)

    n_need = (nv + (sub - 1)) // sub

    @pl.when(n_need == n_sub)
    def _():
        swiglu_rows(slice(None))

    for sb in range(n_sub - 1):
        @pl.when(jnp.logical_and(n_need < n_sub, sb < n_need))
        def _():
            swiglu_rows(slice(sb * sub, (sb + 1) * sub))


def ffn_call(x, w_gate, w_up, w_down, wl, tile_expert, tile_x, tile_valid, *, tm, tf=512, n_sub=4):
    P, D = x.shape
    F = w_gate.shape[-1]
    tf = min(tf, F)
    nf = F // tf
    n_tiles = P // tm

    def fidx(i, f, act):
        return jnp.where(act[i] > 0, f, nf - 1)

    grid_spec = pltpu.PrefetchScalarGridSpec(
        num_scalar_prefetch=3,
        grid=(n_tiles, nf),
        in_specs=[
            pl.BlockSpec((tm, D), lambda i, f, te, xi, act: (xi[i], 0), pipeline_mode=pl.Buffered(1)),
            pl.BlockSpec((None, None, D, tf), lambda i, f, te, xi, act: (wl, te[i], 0, fidx(i, f, act))),
            pl.BlockSpec((None, None, D, tf), lambda i, f, te, xi, act: (wl, te[i], 0, fidx(i, f, act))),
            pl.BlockSpec((None, None, tf, D), lambda i, f, te, xi, act: (wl, te[i], fidx(i, f, act), 0)),
        ],
        out_specs=pl.BlockSpec((tm, D), lambda i, f, te, xi, act: (i, 0), pipeline_mode=pl.Buffered(1)),
    )
    return pl.pallas_call(
        functools.partial(_ffn_kernel, n_sub=n_sub),
        out_shape=jax.ShapeDtypeStruct((P, D), F32),
        grid_spec=grid_spec,
        compiler_params=_cparams(("arbitrary", "arbitrary")), name="ffn",
    )(tile_expert, tile_x, tile_valid, x, w_gate, w_up, w_down)


def _split3(x):
    hi = x.astype(BF16)
    r1 = x - hi.astype(F32)
    mid = r1.astype(BF16)
    lo = (r1 - mid.astype(F32)).astype(BF16)
    return hi, mid, lo


def _router_kernel(h_ref, wr_ref, tri_ref, idx_ref, gate_ref, cnt_ref, carry_s, *, tm):
    i = pl.program_id(0)

    @pl.when(i == 0)
    def _():
        carry_s[...] = jnp.zeros(carry_s.shape, F32)

    hs = _split3(_rows_2d(h_ref))
    ws = _split3(wr_ref[...])
    logits = jnp.zeros((N_EXPERTS, tm), F32)
    for a in range(3):
        for b in range(3 - a):
            logits += lax.dot_general(ws[a], hs[b], NT_DIMS, preferred_element_type=F32)
    eid = lax.broadcasted_iota(I32, (N_EXPERTS, tm), 0).astype(F32)
    none = float(N_EXPERTS)
    m1 = jnp.max(logits, axis=0, keepdims=True)
    e1 = jnp.min(jnp.where(logits == m1, eid, none), axis=0, keepdims=True)
    is1 = eid == e1
    rest = jnp.where(is1, -jnp.inf, logits)
    m2 = jnp.max(rest, axis=0, keepdims=True)
    e2 = jnp.min(jnp.where(rest == m2, eid, none), axis=0, keepdims=True)
    is2 = eid == e2
    ex = jnp.exp(m2 - m1)
    g1 = 1.0 / (1.0 + ex)
    g2 = ex / (1.0 + ex)
    sel = jnp.where(jnp.logical_or(is1, is2), 1.0, 0.0)
    before = jnp.dot(sel.astype(BF16), tri_ref[...], preferred_element_type=F32) + carry_s[...]
    r1 = jnp.sum(jnp.where(is1, before, 0.0), axis=0, keepdims=True)
    r2 = jnp.sum(jnp.where(is2, before, 0.0), axis=0, keepdims=True)
    carry_s[...] = carry_s[...] + jnp.sum(sel, axis=1, keepdims=True)
    idx_ref[0:1, :] = e1.astype(I32)
    idx_ref[1:2, :] = e2.astype(I32)
    idx_ref[2:3, :] = r1.astype(I32)
    idx_ref[3:4, :] = r2.astype(I32)
    idx_ref[4:8, :] = jnp.zeros((4, tm), I32)
    gate_ref[0:1, :] = g1
    gate_ref[1:2, :] = g2
    gate_ref[2:8, :] = jnp.zeros((6, tm), F32)
    cnt_ref[...] = jnp.broadcast_to(carry_s[...], cnt_ref.shape).astype(I32)


def router_call(h, w_router, wl, *, tm=512):
    T, D = h.shape[0], h.shape[1] * h.shape[2]
    tm = _fit_tile(T, tm)
    n = T // tm
    wr_t = jnp.swapaxes(w_router, 1, 2)
    tri = (jnp.arange(tm)[:, None] < jnp.arange(tm)[None, :]).astype(BF16)
    idx, gate, cnt = pl.pallas_call(
        functools.partial(_router_kernel, tm=tm),
        out_shape=[jax.ShapeDtypeStruct((n, 8, tm), I32),
                   jax.ShapeDtypeStruct((n, 8, tm), F32),
                   jax.ShapeDtypeStruct((N_EXPERTS, 128), I32)],
        grid=(n,),
        in_specs=[
            pl.BlockSpec((tm,) + h.shape[1:], lambda i: (i, 0, 0)),
            pl.BlockSpec((None, N_EXPERTS, D), lambda i: (wl, 0, 0)),
            pl.BlockSpec((tm, tm), lambda i: (0, 0)),
        ],
        out_specs=[
            pl.BlockSpec((None, 8, tm), lambda i: (i, 0, 0)),
            pl.BlockSpec((None, 8, tm), lambda i: (i, 0, 0)),
            pl.BlockSpec((N_EXPERTS, 128), lambda i: (0, 0)),
        ],
        scratch_shapes=[pltpu.VMEM((N_EXPERTS, 1), F32)],
        compiler_params=_cparams(("arbitrary",)), name="router",
    )(h, wr_t, tri)
    rows = lambda a, k: a[:, k, :].reshape(T)
    return (rows(idx, 0), rows(idx, 1), rows(idx, 2), rows(idx, 3),
            rows(gate, 0), rows(gate, 1), cnt[:, 0])


def _start_row_copies(src_hbm, idx_ref, base, dst, sem, n):
    def body(r, carry):
        pltpu.make_async_copy(src_hbm.at[pl.ds(idx_ref[base + r], 1)],
                              dst.at[pl.ds(r, 1)], sem).start()
        return carry
    lax.fori_loop(0, n, body, 0, unroll=8)


def _wait_row_copies(src_hbm, dst, sem, n):
    pltpu.make_async_copy(src_hbm.at[pl.ds(0, n)], dst, sem).wait()


def _gather_kernel(src_ref, nact_ref, h_hbm, o_ref, buf, sem, *, tg):
    i = pl.program_id(0)
    nact = nact_ref[0]
    slot = i % 2

    def start(blk, s):
        _start_row_copies(h_hbm, src_ref, blk * tg, buf.at[s], sem.at[s], tg)

    @pl.when(jnp.logical_and(i == 0, nact > 0))
    def _():
        start(0, 0)

    @pl.when(i + 1 < nact)
    def _():
        start(i + 1, 1 - slot)

    @pl.when(i < nact)
    def _():
        _wait_row_copies(h_hbm, buf.at[slot], sem.at[slot], tg)
        o_ref[...] = _rows_2d(buf, slot).astype(o_ref.dtype)

    @pl.when(i >= nact)
    def _():
        o_ref[...] = jnp.zeros(o_ref.shape, o_ref.dtype)


def gather_call(h, src_token, n_active_blocks, *, n_slots, tg=256):
    D = h.shape[1] * h.shape[2]
    grid_spec = pltpu.PrefetchScalarGridSpec(
        num_scalar_prefetch=2,
        grid=(n_slots // tg,),
        in_specs=[pl.BlockSpec(memory_space=pl.ANY)],
        out_specs=pl.BlockSpec((tg, D), lambda i, src, nact: (i, 0)),
        scratch_shapes=[pltpu.VMEM((2, tg) + h.shape[1:], F32), pltpu.SemaphoreType.DMA((2,))],
    )
    return pl.pallas_call(
        functools.partial(_gather_kernel, tg=tg),
        out_shape=jax.ShapeDtypeStruct((n_slots, D), BF16),
        grid_spec=grid_spec,
        compiler_params=_cparams(("arbitrary",)), name="gather",
    )(src_token, n_active_blocks, h)


def _combine_kernel(s1_ref, s2_ref, y_hbm, g1_ref, g2_ref, x_ref, gpost_ref, gtp, gts, op_ref, os_ref,
                    buf, sem, *, tc, n_tiles, n_ptiles):
    i = pl.program_id(0)
    slot = i % 2

    def start(t, s):
        _start_row_copies(y_hbm, s1_ref, t * tc, buf.at[s, 0], sem.at[s, 0], tc)
        _start_row_copies(y_hbm, s2_ref, t * tc, buf.at[s, 1], sem.at[s, 1], tc)

    @pl.when(i == 0)
    def _():
        start(0, 0)

    @pl.when(i + 1 < n_tiles)
    def _():
        start(i + 1, 1 - slot)

    _wait_row_copies(y_hbm, buf.at[slot, 0], sem.at[slot, 0], tc)
    _wait_row_copies(y_hbm, buf.at[slot, 1], sem.at[slot, 1], tc)
    f = g1_ref[...] * buf[slot, 0] + g2_ref[...] * buf[slot, 1]
    out = x_ref[...] + _mod_value(i, n_ptiles, gtp, gts) * _rms(f, gpost_ref[...])

    @pl.when(i < n_ptiles)
    def _():
        op_ref[...] = out

    @pl.when(i >= n_ptiles)
    def _():
        os_ref[...] = out


def combine_call(geom, y, slot1, slot2, g1, g2, x, g_post, modp, mods, l, gt_col):
    T, D = x.shape
    tc = geom.tm
    npt = geom.n_ptiles
    row = pl.BlockSpec((tc, D), lambda i, *_: (i, 0))
    col = pl.BlockSpec((tc, 1), lambda i, *_: (i, 0))
    grid_spec = pltpu.PrefetchScalarGridSpec(
        num_scalar_prefetch=2,
        grid=(geom.n_tiles,),
        in_specs=[pl.BlockSpec(memory_space=pl.ANY), col, col, row,
                  pl.BlockSpec((None, 1, D), lambda i, *_: (l, 0, 0))]
                 + _mod_specs(geom, D, l, gt_col),
        out_specs=[pl.BlockSpec((tc, D), lambda i, *_: (jnp.minimum(i, npt - 1), 0)),
                   pl.BlockSpec((tc, D), lambda i, *_: (jnp.maximum(i - npt, 0), 0))],
        scratch_shapes=[pltpu.VMEM((2, 2, tc, D), F32), pltpu.SemaphoreType.DMA((2, 2))],
    )
    return pl.pallas_call(
        functools.partial(_combine_kernel, tc=tc, n_tiles=geom.n_tiles, n_ptiles=geom.n_ptiles),
        out_shape=[jax.ShapeDtypeStruct((npt * tc, D), F32),
                   jax.ShapeDtypeStruct((T - npt * tc, D), F32)],
        grid_spec=grid_spec,
        compiler_params=_cparams(("arbitrary",)), name="combine",
    )(slot1, slot2, y, g1.reshape(T, 1), g2.reshape(T, 1), x, g_post, modp, mods)


def moe_dispatch_plan(e1, e2, r1, r2, counts, *, tm, n_tiles_max):
    T = e1.shape[0]
    tiles_e = (counts + tm - 1) // tm
    tile_end = jnp.cumsum(tiles_e)
    tile_off = tile_end - tiles_e
    n_active = tile_end[-1]
    slot1 = tile_off[e1] * tm + r1
    slot2 = tile_off[e2] * tm + r2
    tok = jnp.arange(T, dtype=I32)
    src = jnp.zeros((n_tiles_max * tm,), I32).at[slot1].set(tok).at[slot2].set(tok)
    t = jnp.arange(n_tiles_max, dtype=I32)
    tile_x = jnp.minimum(t, n_active - 1).astype(I32)
    tile_expert = jnp.minimum(jnp.sum(tile_end[None, :] <= tile_x[:, None], axis=1), N_EXPERTS - 1).astype(I32)
    rows_left = counts[tile_expert] - (tile_x - tile_off[tile_expert]) * tm
    tile_valid = jnp.where(t < n_active, jnp.clip(rows_left, 0, tm), 0).astype(I32)
    return slot1.astype(I32), slot2.astype(I32), src, tile_expert, tile_x, tile_valid, n_active.astype(I32)


def kernel(x_prompt, x_sample, c_prompt, c_sample, cache_k, cache_v, page_table, g_pre_mix, g_post_mix, g_pre_ffn, g_post_ffn, w_ada, b_ada, w_in, w_out, lam_q1, lam_k1, lam_q2, lam_k2, g_subln, gm_ws, gm_b, gm_norm_g, gm_norm_b, w_d_gate, w_d_up, w_d_down, w_router, w_e_gate, w_e_up, w_e_down):
    B, S, D = x_prompt.shape
    Bd, Td, _ = x_sample.shape
    L = w_in.shape[0]
    NP, NS = B * S, Bd * Td
    T = NP + NS
    H = DA_HEADS
    QKW = 2 * DA_HEAD_DIM * H
    VW = DA_VDIM * H
    GW = GM_HEADS * GM_DIM
    assert S % CHUNK == 0 and NS % CHUNK == 0 and CHUNK % Td == 0

    tm_norm = min(256, NS)
    geom = RowGeom(B, S, Bd, Td, tm_norm)
    tm_ffn = min(1024, NS)
    assert T % tm_ffn == 0

    x = jnp.concatenate([x_prompt.reshape(NP, D), x_sample.reshape(NS, D)], axis=0)
    n_c = B + Bd
    c_all = jnp.concatenate([c_prompt, c_sample, jnp.zeros((-n_c % 8, D), F32)], axis=0)
    mod = ada_call(c_all, w_ada, b_ada)
    modp = mod[:, :B].reshape(L, B, 1, 6 * D)
    mods = jnp.repeat(mod[:, B:n_c], Td, axis=1)
    SH_M, SC_M, GT_M, SH_F, SC_F, GT_F = range(6)

    g3 = lambda a: a.reshape(L, 1, a.shape[-1])
    g_pre_mix3, g_post_mix3, g_pre_ffn3, g_post_ffn3 = map(g3, (g_pre_mix, g_post_mix, g_pre_ffn, g_post_ffn))
    ng3 = gm_norm_g.reshape(L, 1, GW)
    nb3 = gm_norm_b.reshape(L, 1, GW)
    lamv = jnp.stack([lam_q1, lam_k1, lam_q2, lam_k2], axis=1)

    n_dense_tiles = T // tm_ffn
    dense_te = jnp.zeros((n_dense_tiles,), I32)
    dense_xi = jnp.arange(n_dense_tiles, dtype=I32)
    dense_act = jnp.full((n_dense_tiles,), tm_ffn, I32)

    (h,) = norm_call(geom, x, modp, mods, g_pre=g_pre_mix3, l_pre=0, sc_col=SC_M, sh_col=SH_M)

    k_p, v_p, k_s, v_s, gv_s = [], [], [], [], []
    for l in range(L):
        lam_init = 0.8 - 0.6 * math.exp(-0.3 * l)
        z, kt = proj_in_call(h, w_in, l, n_batch=B, seq=S, k_col0=QKW, k_width=QKW)
        k_p.append(jnp.transpose(kt.reshape(B, H, 2, DA_HEAD_DIM, S), (0, 4, 1, 2, 3)))
        v_p.append(z[:NP, 2 * QKW:2 * QKW + VW].reshape(B, S, H, DA_VDIM))
        k_s.append(z[NP:, QKW:2 * QKW].reshape(Bd, Td, H, 2, DA_HEAD_DIM))
        v_s.append(z[NP:, 2 * QKW:2 * QKW + VW].reshape(Bd, Td, H, DA_VDIM))

        o_att_p, o_att_s = attn_call(z, cache_k, cache_v, page_table, lamv, g_subln, l,
                                     n_batch=B, seq=S, dec_len=Td, lam_init=lam_init)

        reps = CHUNK // Td
        wmix = jnp.stack([gm_ws[l], jnp.tile(gm_ws[l][:, :Td, :Td], (1, reps, reps))])
        bmix = jnp.stack([gm_b[l].T, jnp.tile(gm_b[l][:, :Td], (1, reps)).T])
        o_gm, gvn = chunk_mlp_call(z, wmix, bmix, ng3, nb3, l, n_ptiles=NP // CHUNK, dec_len=Td,
                                   col0=2 * QKW + VW)
        gv_s.append(gvn[NP:].reshape(Bd, Td, GM_HEADS, GM_DIM))

        m = mm_call([(o_att_p, o_att_s), o_gm], w_out, l)
        is_moe = l % 2 == 1
        x, h2 = norm_call(geom, x, modp, mods, m=m, g_post=g_post_mix3, l_post=l, gt_col=GT_M,
                          g_pre=g_pre_ffn3, l_pre=l, sc_col=SC_F, sh_col=SH_F,
                          h_dtype=F32 if is_moe else BF16, h_row_major=is_moe)
        wl = l // 2
        if not is_moe:
            f = ffn_call(h2, w_d_gate[:, None], w_d_up[:, None], w_d_down[:, None], wl,
                         dense_te, dense_xi, dense_act, tm=tm_ffn)
            if l + 1 < L:
                x, h = norm_call(geom, x, modp, mods, m=f, g_post=g_post_ffn3, l_post=l, gt_col=GT_F,
                                 g_pre=g_pre_mix3, l_pre=l + 1, sc_col=SC_M, sh_col=SH_M)
            else:
                (x,) = norm_call(geom, x, modp, mods, m=f, g_post=g_post_ffn3, l_post=l, gt_col=GT_F)
        else:
            e1, e2, r1, r2, g1, g2, counts = router_call(h2, w_router, wl)
            n_tiles_max = (2 * T + N_EXPERTS * (tm_ffn - 1)) // tm_ffn
            slot1, slot2, src, te, xi, act, n_active = moe_dispatch_plan(
                e1, e2, r1, r2, counts, tm=tm_ffn, n_tiles_max=n_tiles_max)
            tg = min(256, tm_ffn)
            xs = gather_call(h2, src, (n_active * (tm_ffn // tg)).reshape(1),
                             n_slots=n_tiles_max * tm_ffn, tg=tg)
            y = ffn_call(xs, w_e_gate, w_e_up, w_e_down, wl, te, xi, act, tm=tm_ffn)
            x_halves = combine_call(geom, y, slot1, slot2, g1, g2, x, g_post_ffn3, modp, mods, l, GT_F)
            x = None
            if l + 1 < L:
                x = jnp.concatenate(x_halves, axis=0)
                (h,) = norm_call(geom, x, modp, mods, g_pre=g_pre_mix3, l_pre=l + 1, sc_col=SC_M, sh_col=SH_M)

    if x is not None:
        x_halves = (x[:NP], x[NP:])
    y_prompt = x_halves[0].reshape(B, S, D)
    y_sample = x_halves[1].reshape(Bd, Td, D)
    return (y_prompt, y_sample, jnp.stack(k_p), jnp.stack(v_p), jnp.stack(k_s), jnp.stack(v_s),
            jnp.stack(gv_s))
```

```python
import functools
import math

import jax
import jax.numpy as jnp
from jax import lax
from jax.experimental import pallas as pl
from jax.experimental.pallas import tpu as pltpu

F32 = jnp.float32
BF16 = jnp.bfloat16
I32 = jnp.int32

EPS = 1e-6
DA_HEADS = 8
DA_HEAD_DIM = 64
DA_VDIM = 128
GM_HEADS = 8
GM_DIM = 128
CHUNK = 128
PAGE_SIZE = 128
N_EXPERTS = 8
NEG = -1e30
LOG2E = 1.4426950408889634

V7X_VMEM_BYTES = 64 * 1024 * 1024
VMEM_LIMIT = V7X_VMEM_BYTES - 8 * 1024 * 1024

NT_DIMS = (((1,), (1,)), ((), ()))

LANES = 128


def _rows_2d(ref3, *lead):
    return jnp.concatenate([ref3[(*lead, slice(None), j, slice(None))] for j in range(ref3.shape[-2])], axis=1)


KEY_PIECE = 256
QUERY_PIECE = 128


def _cparams(sem):
    return pltpu.CompilerParams(dimension_semantics=sem, vmem_limit_bytes=VMEM_LIMIT)


def _fit_tile(n, t):
    while n % t:
        t //= 2
    return t


def _rms(x, g):
    return x * lax.rsqrt(jnp.mean(x * x, axis=-1, keepdims=True) + EPS) * g


def _ada_kernel(c_ref, w_ref, b_ref, o_ref):
    c = c_ref[...]
    a = (c * jax.nn.sigmoid(c)).astype(BF16)
    o_ref[...] = jnp.dot(a, w_ref[...].astype(BF16), preferred_element_type=F32) + b_ref[...]


def ada_call(c_all, w_ada, b_ada):
    R, D = c_all.shape
    L, _, N = w_ada.shape
    tn = min(1024, N)
    return pl.pallas_call(
        _ada_kernel,
        out_shape=jax.ShapeDtypeStruct((L, R, N), F32),
        grid=(L, N // tn),
        in_specs=[
            pl.BlockSpec((R, D), lambda l, j: (0, 0)),
            pl.BlockSpec((None, D, tn), lambda l, j: (l, 0, j)),
            pl.BlockSpec((None, 1, tn), lambda l, j: (l, 0, j)),
        ],
        out_specs=pl.BlockSpec((None, R, tn), lambda l, j: (l, 0, j)),
        compiler_params=_cparams(("arbitrary", "arbitrary")),
        name="ada",
    )(c_all, w_ada, b_ada.reshape(L, 1, N))


class RowGeom:
    def __init__(self, n_batch, seq, n_dec, dec_len, tm):
        assert seq % tm == 0 and (n_dec * dec_len) % tm == 0
        self.tm = tm
        self.n_batch = n_batch
        self.tiles_per_batch = seq // tm
        self.n_ptiles = n_batch * self.tiles_per_batch
        self.n_tiles = self.n_ptiles + (n_dec * dec_len) // tm


def _mod_specs(geom, D, l, col):
    tpb, npt, nb = geom.tiles_per_batch, geom.n_ptiles, geom.n_batch
    sp = pl.BlockSpec((None, None, 1, D),
                      lambda i, *_: (l, jnp.minimum(i // tpb, nb - 1), 0, col))
    ss = pl.BlockSpec((None, geom.tm, D),
                      lambda i, *_: (l, jnp.maximum(i - npt, 0), col))
    return [sp, ss]


def _mod_value(i, n_ptiles, p_ref, s_ref):
    return jnp.where(i < n_ptiles, p_ref[...], s_ref[...])


def _norm_kernel(*refs, n_ptiles, post, pre):
    it = iter(refs)
    x_ref = next(it)
    if post:
        m_ref, gpost_ref, gtp, gts = next(it), next(it), next(it), next(it)
    if pre:
        gpre_ref, scp, scs, shp, shs = next(it), next(it), next(it), next(it), next(it)
    if post:
        xo_ref = next(it)
    if pre:
        h_ref = next(it)
    i = pl.program_id(0)
    x = x_ref[...]
    if post:
        x = x + _mod_value(i, n_ptiles, gtp, gts) * _rms(m_ref[...], gpost_ref[...])
        xo_ref[...] = x
    if pre:
        h = _rms(x, gpre_ref[...]) * (1.0 + _mod_value(i, n_ptiles, scp, scs)) \
            + _mod_value(i, n_ptiles, shp, shs)
        h_ref[...] = h.astype(h_ref.dtype).reshape(h_ref.shape)


def norm_call(geom, x, modp, mods, *, m=None, g_post=None, l_post=None, gt_col=None,
              g_pre=None, l_pre=None, sc_col=None, sh_col=None, h_dtype=BF16, h_row_major=False):
    T, D = x.shape
    tm = geom.tm
    post, pre = m is not None, g_pre is not None
    row = pl.BlockSpec((tm, D), lambda i: (i, 0))
    args, specs = [x], [row]
    if post:
        args += [m, g_post, modp, mods]
        specs += [row, pl.BlockSpec((None, 1, D), lambda i: (l_post, 0, 0))]
        specs += _mod_specs(geom, D, l_post, gt_col)
    if pre:
        args += [g_pre, modp, mods, modp, mods]
        specs += [pl.BlockSpec((None, 1, D), lambda i: (l_pre, 0, 0))]
        specs += _mod_specs(geom, D, l_pre, sc_col) + _mod_specs(geom, D, l_pre, sh_col)
    out_shape, out_specs = [], []
    if post:
        out_shape.append(jax.ShapeDtypeStruct((T, D), F32))
        out_specs.append(row)
    if pre and h_row_major:
        out_shape.append(jax.ShapeDtypeStruct((T, D // LANES, LANES), h_dtype))
        out_specs.append(pl.BlockSpec((tm, D // LANES, LANES), lambda i: (i, 0, 0)))
    elif pre:
        out_shape.append(jax.ShapeDtypeStruct((T, D), h_dtype))
        out_specs.append(row)
    outs = pl.pallas_call(
        functools.partial(_norm_kernel, n_ptiles=geom.n_ptiles, post=post, pre=pre),
        out_shape=out_shape, grid=(geom.n_tiles,), in_specs=specs, out_specs=out_specs,
        compiler_params=_cparams(("arbitrary",)), name="norm",
    )(*args)
    return outs


def _mm_kernel(*refs, split, n_head_tiles):
    it = iter(refs)
    a_refs = [(next(it), next(it)) if s else next(it) for s in split]
    w_refs = [next(it) for _ in split]
    o_ref = next(it)
    i = pl.program_id(0)
    acc = None
    for a_ref, w_ref in zip(a_refs, w_refs):
        a = jnp.where(i < n_head_tiles, a_ref[0][...], a_ref[1][...]) if isinstance(a_ref, tuple) else a_ref[...]
        d = jnp.dot(a.astype(BF16), w_ref[...].astype(BF16), preferred_element_type=F32)
        acc = d if acc is None else acc + d
    o_ref[...] = acc


def mm_call(a_list, w, l, *, tm=1024, tn=512):
    split = [isinstance(a, tuple) for a in a_list]
    rows = lambda a: a[0].shape[0] + a[1].shape[0] if isinstance(a, tuple) else a.shape[0]
    width = lambda a: a[0].shape[1] if isinstance(a, tuple) else a.shape[1]
    T = rows(a_list[0])
    N = w.shape[2]
    tm, tn = _fit_tile(T, tm), _fit_tile(N, tn)
    n_head_tiles = 0
    args, specs = [], []
    for a in a_list:
        if isinstance(a, tuple):
            tm = _fit_tile(a[1].shape[0], _fit_tile(a[0].shape[0], tm))
    for a in a_list:
        if isinstance(a, tuple):
            nh = a[0].shape[0] // tm
            assert n_head_tiles in (0, nh)
            n_head_tiles = nh
            args += [a[0], a[1]]
            specs += [pl.BlockSpec((tm, width(a)), lambda i, j, nh=nh: (jnp.minimum(i, nh - 1), 0)),
                      pl.BlockSpec((tm, width(a)), lambda i, j, nh=nh: (jnp.maximum(i - nh, 0), 0))]
        else:
            args.append(a)
            specs.append(pl.BlockSpec((tm, width(a)), lambda i, j: (i, 0)))
    for k, a in enumerate(a_list):
        specs.append(pl.BlockSpec((None, width(a), tn), lambda i, j, k=k: (l, k, j)))
    return pl.pallas_call(
        functools.partial(_mm_kernel, split=tuple(split), n_head_tiles=n_head_tiles),
        out_shape=jax.ShapeDtypeStruct((T, N), F32),
        grid=(T // tm, N // tn), in_specs=specs,
        out_specs=pl.BlockSpec((tm, tn), lambda i, j: (i, j)),
        compiler_params=_cparams(("arbitrary", "arbitrary")), name="matmul",
    )(*args, *([w] * len(a_list)))


def _proj_in_kernel(h_ref, w_ref, z_ref, kt_ref, *, n_head_tiles, j0, nk):
    i, j = pl.program_id(0), pl.program_id(1)
    acc = jnp.dot(h_ref[...], w_ref[...].astype(BF16), preferred_element_type=F32)
    z_ref[...] = acc

    @pl.when(jnp.logical_and(jnp.logical_and(j >= j0, j < j0 + nk), i < n_head_tiles))
    def _():
        kt_ref[...] = acc.T


def proj_in_call(h, w, l, *, n_batch, seq, k_col0, k_width, tm=1024, tn=512):
    T, D = h.shape
    N = w.shape[2]
    n_head_rows = n_batch * seq
    tm = _fit_tile(seq, _fit_tile(T - n_head_rows, _fit_tile(T, tm)))
    tn = _fit_tile(k_width, _fit_tile(k_col0, _fit_tile(N, tn)))
    n_tiles, n_head, tpb = T // tm, n_head_rows // tm, seq // tm
    j0, nk = k_col0 // tn, k_width // tn
    row = lambda i: i

    def kt_index(i, j):
        head = i < n_head
        pt = jnp.where(head, i, n_head - 1)
        return (pt // tpb, jnp.where(head, jnp.clip(j - j0, 0, nk - 1), nk - 1), pt % tpb)

    return pl.pallas_call(
        functools.partial(_proj_in_kernel, n_head_tiles=n_head, j0=j0, nk=nk),
        out_shape=[jax.ShapeDtypeStruct((T, N), F32),
                   jax.ShapeDtypeStruct((n_batch, k_width, seq), F32)],
        grid=(n_tiles, N // tn),
        in_specs=[pl.BlockSpec((tm, D), lambda i, j: (row(i), 0)),
                  pl.BlockSpec((None, D, tn), lambda i, j: (l, 0, j))],
        out_specs=[pl.BlockSpec((tm, tn), lambda i, j: (row(i), j)),
                   pl.BlockSpec((None, tn, tm), kt_index)],
        compiler_params=_cparams(("arbitrary", "arbitrary")), name="proj_in",
    )(h, w)


def _lam_value(lamv_ref, lam_init):
    v = lamv_ref[...]
    s1 = jnp.sum(v[0:1] * v[1:2], axis=-1, keepdims=True)
    s2 = jnp.sum(v[2:3] * v[3:4], axis=-1, keepdims=True)
    return jnp.exp(s1) - jnp.exp(s2) + lam_init


def _alibi_slope(head):
    return lax.bitcast_convert_type((126 - head) << 23, F32)


def _attn_p_kernel(qi_tab, ki_tab, lamv_ref, q_ref, k_ref, v_ref, g_ref, o_ref,
                   q_s, m_s, l_s, acc_s, *, tq, lam_init):
    h = pl.program_id(1)
    p = pl.program_id(2)
    qi, ki = qi_tab[p], ki_tab[p]

    @pl.when(ki == 0)
    def _():
        q = q_ref[...] * (DA_HEAD_DIM ** -0.5 * LOG2E)
        lane = lax.broadcasted_iota(I32, q.shape, 1)
        q_s[0] = jnp.where(lane < DA_HEAD_DIM, q, 0.0).astype(BF16)
        q_s[1] = jnp.where(lane >= DA_HEAD_DIM, q, 0.0).astype(BF16)
        m_s[...] = jnp.full(m_s.shape, NEG, F32)
        l_s[...] = jnp.zeros(l_s.shape, F32)
        acc_s[...] = jnp.zeros(acc_s.shape, F32)

    def update(masked):
        kb, qb = min(KEY_PIECE, tq), min(QUERY_PIECE, tq)
        k = k_ref[...].astype(BF16)
        vt = v_ref[...].T.astype(BF16)
        slope = _alibi_slope(jnp.zeros((kb, qb), I32) + h) * LOG2E
        krows = [lax.broadcasted_iota(I32, (kb, qb), 0) + k0 for k0 in range(0, tq, kb)]
        biases = [slope * ((ki - qi) * tq + kr).astype(F32) for kr in krows]
        for q0 in range(0, tq, qb):
            cols = slice(q0, q0 + qb)
            qcol = lax.broadcasted_iota(I32, (kb, qb), 1) + q0
            for mi in range(2):
                m, l, acc = m_s[mi, :, cols], l_s[mi, :, cols], acc_s[mi, :, cols]
                for kp, k0 in enumerate(range(0, tq, kb)):
                    if masked and k0 > q0 + qb - 1:
                        continue
                    b = biases[kp]
                    if masked and k0 + kb - 1 > q0:
                        b = jnp.where(krows[kp] <= qcol, b, NEG)
                    s = lax.dot_general(k[k0:k0 + kb], q_s[mi, cols, :], NT_DIMS,
                                        preferred_element_type=F32) + b
                    m_new = jnp.maximum(m, jnp.max(s, axis=0, keepdims=True))
                    alpha = jnp.exp2(m - m_new)
                    pm = jnp.exp2(s - m_new)
                    l = alpha * l + jnp.sum(pm, axis=0, keepdims=True)
                    acc = alpha * acc + jnp.dot(vt[:, k0:k0 + kb], pm.astype(BF16),
                                                preferred_element_type=F32)
                    m = m_new
                m_s[mi, :, cols], l_s[mi, :, cols], acc_s[mi, :, cols] = m, l, acc

    @pl.when(ki < qi)
    def _():
        update(False)

    @pl.when(ki == qi)
    def _():
        update(True)
        lam = _lam_value(lamv_ref, lam_init)
        o = acc_s[0] / l_s[0] - lam * (acc_s[1] / l_s[1])
        ms = jnp.mean(o * o, axis=0, keepdims=True)
        on = o * lax.rsqrt(ms + EPS) * g_ref[...] * (1.0 - lam_init)
        o_ref[...] = on.T


def _attn_s_body(b, pt_ref, lamv_ref, q_ref, kn_ref, vn_ref, g_ref, ckt_hbm, cv_hbm, o_ref,
                 kbuf, vbuf, sem, *, layer, n_dec, n_pages, dec_len, lam_init):
    H, W = DA_HEADS, 2 * DA_HEAD_DIM * DA_HEADS
    R = 2 * H * dec_len
    past = n_pages * PAGE_SIZE
    slot = b % 2

    def start_pages(seq, s):
        for pg in range(n_pages):
            page = pt_ref[seq * n_pages + pg]
            pltpu.make_async_copy(ckt_hbm.at[layer, page], kbuf.at[s, pg], sem.at[s, 0]).start(priority=1)
            pltpu.make_async_copy(cv_hbm.at[layer, page], vbuf.at[s, pg], sem.at[s, 1]).start(priority=1)

    @pl.when(b == 0)
    def _():
        start_pages(0, 0)

    @pl.when(b + 1 < n_dec)
    def _():
        start_pages(b + 1, 1 - slot)

    pltpu.make_async_copy(ckt_hbm.at[layer, pl.ds(0, n_pages)], kbuf.at[slot], sem.at[slot, 0]).wait()
    pltpu.make_async_copy(cv_hbm.at[layer, pl.ds(0, n_pages)], vbuf.at[slot], sem.at[slot, 1]).wait()

    q = q_ref[...] * (DA_HEAD_DIM ** -0.5 * LOG2E)
    qt = jnp.broadcast_to(q[None], (2 * H, dec_len, W)).reshape(R, W)
    r = lax.broadcasted_iota(I32, (R, W), 0)
    c = lax.broadcasted_iota(I32, (R, W), 1)
    row_map, row_head = r // (H * dec_len), (r // dec_len) % H
    col_head, col_map = c // (2 * DA_HEAD_DIM), (c // DA_HEAD_DIM) % 2
    keep = jnp.logical_and(row_map == col_map, row_head == col_head)
    qx = jnp.where(keep, qt, 0.0).astype(BF16)
    lane = lax.broadcasted_iota(I32, (R, PAGE_SIZE), 1)
    rr = lax.broadcasted_iota(I32, (R, PAGE_SIZE), 0)
    slope = _alibi_slope((rr // dec_len) % H) * LOG2E
    tiles = []
    for pg in range(n_pages):
        kt = kbuf[slot, pg].astype(BF16)
        bias = slope * (lane + (pg * PAGE_SIZE - past)).astype(F32)
        tiles.append(jnp.dot(qx, kt, preferred_element_type=F32) + bias)
    zpad = jnp.zeros((PAGE_SIZE - dec_len, W), F32)
    kn = jnp.concatenate([kn_ref[...], zpad], axis=0).astype(BF16)
    s_new = lax.dot_general(qx, kn, NT_DIMS, preferred_element_type=F32) + slope * lane.astype(F32)
    tiles.append(jnp.where(lane <= rr % dec_len, s_new, NEG))

    m = tiles[0]
    for t in tiles[1:]:
        m = jnp.maximum(m, t)
    m = jnp.max(m, axis=-1, keepdims=True)
    ps = [jnp.exp2(t - m) for t in tiles]
    lsum = ps[0]
    for t in ps[1:]:
        lsum = lsum + t
    inv = 1.0 / jnp.sum(lsum, axis=-1, keepdims=True)

    lam = _lam_value(lamv_ref, lam_init)
    hq = H * dec_len
    for h in range(H):
        rows1 = slice(h * dec_len, (h + 1) * dec_len)
        rows2 = slice(hq + h * dec_len, hq + (h + 1) * dec_len)
        cols = slice(h * DA_VDIM, (h + 1) * DA_VDIM)
        ph = jnp.concatenate(
            [jnp.concatenate([t[rows1], t[rows2]], axis=0).astype(BF16) for t in ps], axis=1)
        vparts = [vbuf[slot, pg, pl.ds(h, PAGE_SIZE, stride=H), :].astype(BF16) for pg in range(n_pages)]
        vparts.append(jnp.concatenate([vn_ref[:, cols], zpad[:, :DA_VDIM]], axis=0).astype(BF16))
        acc = jnp.dot(ph, jnp.concatenate(vparts, axis=0), preferred_element_type=F32)
        o = acc[:dec_len] * inv[rows1] - lam * (acc[dec_len:] * inv[rows2])
        o_ref[:, cols] = _rms(o, g_ref[...]) * (1.0 - lam_init)


def _attn_kernel(qi_tab, ki_tab, pt_ref, lamv_ref, q_ref, k_ref, v_ref, gcol_ref,
                 sq_ref, skn_ref, svn_ref, grow_ref, ckt_hbm, cv_hbm, op_ref, os_ref,
                 q_s, m_s, l_s, acc_s, kbuf, vbuf, sem, *, tq, n_pairs, period, sample_kw):
    _attn_p_kernel(qi_tab, ki_tab, lamv_ref, q_ref, k_ref, v_ref, gcol_ref, op_ref,
                   q_s, m_s, l_s, acc_s, tq=tq, lam_init=sample_kw["lam_init"])
    step = (pl.program_id(0) * DA_HEADS + pl.program_id(1)) * n_pairs + pl.program_id(2)
    seq = step // period

    @pl.when(jnp.logical_and(step % period == 0, seq < sample_kw["n_dec"]))
    def _():
        _attn_s_body(seq, pt_ref, lamv_ref, sq_ref, skn_ref, svn_ref, grow_ref, ckt_hbm, cv_hbm, os_ref,
                     kbuf, vbuf, sem, **sample_kw)


def attn_call(z, cache_k, cache_v, page_table, lamv, g_subln, l, *, n_batch, seq, dec_len, lam_init, tq=512):
    n_dec, n_pages = page_table.shape
    H = DA_HEADS
    w = 2 * DA_HEAD_DIM
    W = w * H
    n_layers, n_pool = cache_k.shape[:2]
    ckt = jnp.transpose(cache_k, (0, 1, 3, 4, 5, 2)).reshape(n_layers, n_pool, W, PAGE_SIZE)
    cv = cache_v.reshape(n_layers, n_pool, PAGE_SIZE * H, DA_VDIM)
    tq = min(tq, seq)
    nq = seq // tq
    pairs = [(qi, ki) for qi in range(nq) for ki in range(qi + 1)]
    qi_tab = jnp.asarray([p[0] for p in pairs], I32)
    ki_tab = jnp.asarray([p[1] for p in pairs], I32)
    n_pairs = len(pairs)
    period = (n_batch * H * n_pairs) // n_dec
    assert period >= 1, "fewer prompt steps than sample sequences"
    rb = (n_batch * seq) // dec_len

    def sample_seq(b, h, p):
        return jnp.minimum(((b * H + h) * n_pairs + p) // period, n_dec - 1)

    grid_spec = pltpu.PrefetchScalarGridSpec(
        num_scalar_prefetch=3,
        grid=(n_batch, H, n_pairs),
        in_specs=[
            pl.BlockSpec((None, 4, DA_HEAD_DIM), lambda b, h, p, qt, kt, pt: (l, 0, 0)),
            pl.BlockSpec((tq, w), lambda b, h, p, qt, kt, pt: (b * nq + qt[p], h)),
            pl.BlockSpec((tq, w), lambda b, h, p, qt, kt, pt: (b * nq + kt[p], H + h)),
            pl.BlockSpec((tq, DA_VDIM), lambda b, h, p, qt, kt, pt: (b * nq + kt[p], 2 * H + h)),
            pl.BlockSpec((None, DA_VDIM, 1), lambda b, h, p, qt, kt, pt: (l, 0, 0)),
            pl.BlockSpec((dec_len, W), lambda b, h, p, qt, kt, pt: (rb + sample_seq(b, h, p), 0)),
            pl.BlockSpec((dec_len, W), lambda b, h, p, qt, kt, pt: (rb + sample_seq(b, h, p), 1)),
            pl.BlockSpec((dec_len, W), lambda b, h, p, qt, kt, pt: (rb + sample_seq(b, h, p), 2)),
            pl.BlockSpec((None, 1, DA_VDIM), lambda b, h, p, qt, kt, pt: (l, 0, 0)),
            pl.BlockSpec(memory_space=pl.ANY),
            pl.BlockSpec(memory_space=pl.ANY),
        ],
        out_specs=[
            pl.BlockSpec((tq, DA_VDIM), lambda b, h, p, qt, kt, pt: (b * nq + qt[p], h)),
            pl.BlockSpec((dec_len, W), lambda b, h, p, qt, kt, pt: (sample_seq(b, h, p), 0)),
        ],
        scratch_shapes=[
            pltpu.VMEM((2, tq, w), BF16),
            pltpu.VMEM((2, 1, tq), F32),
            pltpu.VMEM((2, 1, tq), F32),
            pltpu.VMEM((2, DA_VDIM, tq), F32),
            pltpu.VMEM((2, n_pages, W, PAGE_SIZE), F32),
            pltpu.VMEM((2, n_pages, PAGE_SIZE * H, DA_VDIM), F32),
            pltpu.SemaphoreType.DMA((2, 2)),
        ],
    )
    sample_kw = dict(layer=l, n_dec=n_dec, n_pages=n_pages, dec_len=dec_len, lam_init=lam_init)
    return pl.pallas_call(
        functools.partial(_attn_kernel, tq=tq, n_pairs=n_pairs, period=period, sample_kw=sample_kw),
        out_shape=[jax.ShapeDtypeStruct((n_batch * seq, W), F32),
                   jax.ShapeDtypeStruct((n_dec * dec_len, W), F32)],
        grid_spec=grid_spec,
        compiler_params=_cparams(("arbitrary", "arbitrary", "arbitrary")),
        name="attn",
    )(qi_tab, ki_tab, page_table.reshape(-1), lamv, z, z, z, g_subln.reshape(-1, DA_VDIM, 1),
      z, z, z, g_subln.reshape(-1, 1, DA_VDIM), ckt, cv)


def _gm_kernel(gu_ref, gv_ref, w_ref, b_ref, ng_ref, nb_ref, o_ref, gvo_ref, *, n_ptiles, dec_len):
    i = pl.program_id(0)
    r = lax.broadcasted_iota(I32, (CHUNK, CHUNK), 0)
    c = lax.broadcasted_iota(I32, (CHUNK, CHUNK), 1)
    same_seq = jnp.where(r // dec_len == c // dec_len, 1, 0) + jnp.where(i < n_ptiles, 1, 0)
    keep = jnp.logical_and(c <= r, same_seq > 0)
    for g in range(GM_HEADS):
        cols = slice(g * GM_DIM, (g + 1) * GM_DIM)
        v = jax.nn.gelu(gv_ref[:, cols])
        mu = jnp.mean(v, axis=-1, keepdims=True)
        var = jnp.mean(jnp.square(v - mu), axis=-1, keepdims=True)
        vn = (v - mu) * lax.rsqrt(var + EPS) * ng_ref[:, cols] + nb_ref[:, cols]
        gvo_ref[:, cols] = vn
        wm = jnp.where(keep, w_ref[g], 0.0).astype(BF16)
        mixed = jnp.dot(wm, vn.astype(BF16), preferred_element_type=F32) + b_ref[:, g:g + 1]
        o_ref[:, cols] = jax.nn.gelu(gu_ref[:, cols]) * mixed


def chunk_mlp_call(z, wmix, bmix, ng, nb, l, *, n_ptiles, dec_len, col0):
    T = z.shape[0]
    GW = GM_HEADS * GM_DIM
    cb = col0 // GW
    row = pl.BlockSpec((CHUNK, GW), lambda i: (i, 0))
    sel = lambda i: jnp.where(i < n_ptiles, 0, 1)
    return pl.pallas_call(
        functools.partial(_gm_kernel, n_ptiles=n_ptiles, dec_len=dec_len),
        out_shape=[jax.ShapeDtypeStruct((T, GW), F32)] * 2,
        grid=(T // CHUNK,),
        in_specs=[
            pl.BlockSpec((CHUNK, GW), lambda i: (i, cb)),
            pl.BlockSpec((CHUNK, GW), lambda i: (i, cb + 1)),
            pl.BlockSpec((None, GM_HEADS, CHUNK, CHUNK), lambda i: (sel(i), 0, 0, 0)),
            pl.BlockSpec((None, CHUNK, GM_HEADS), lambda i: (sel(i), 0, 0)),
            pl.BlockSpec((None, 1, GW), lambda i: (l, 0, 0)),
            pl.BlockSpec((None, 1, GW), lambda i: (l, 0, 0)),
        ],
        out_specs=[row, row],
        compiler_params=_cparams(("arbitrary",)), name="chunk_mlp",
    )(z, z, wmix, bmix, ng, nb)


def _ffn_kernel(te_ref, xi_ref, nv_ref, x_ref, wg_ref, wu_ref, wd_ref, o_ref, *, n_sub):
    i, f = pl.program_id(0), pl.program_id(1)
    tm = x_ref.shape[0]
    sub = tm // n_sub
    nv = nv_ref[i]

    @pl.when(f == 0)
    def _():
        o_ref[...] = jnp.zeros(o_ref.shape, F32)

    def swiglu_rows(rows):
        x = x_ref[rows, :]
        g = jnp.dot(x, wg_ref[...].astype(BF16), preferred_element_type=F32)
        u = jnp.dot(x, wu_ref[...].astype(BF16), preferred_element_type=F32)
        a = (g * jax.nn.sigmoid(g) * u).astype(BF16)
        o_ref[rows, :] += jnp.dot(a, wd_ref[...].astype(BF16), preferred_element_type=F32)

    n_need = (nv + (sub - 1)) // sub

    @pl.when(n_need == n_sub)
    def _():
        swiglu_rows(slice(None))

    for sb in range(n_sub - 1):
        @pl.when(jnp.logical_and(n_need < n_sub, sb < n_need))
        def _():
            swiglu_rows(slice(sb * sub, (sb + 1) * sub))


def ffn_call(x, w_gate, w_up, w_down, wl, tile_expert, tile_x, tile_valid, *, tm, tf=512, n_sub=4):
    P, D = x.shape
    F = w_gate.shape[-1]
    tf = min(tf, F)
    nf = F // tf
    n_tiles = P // tm

    def fidx(i, f, act):
        return jnp.where(act[i] > 0, f, nf - 1)

    grid_spec = pltpu.PrefetchScalarGridSpec(
        num_scalar_prefetch=3,
        grid=(n_tiles, nf),
        in_specs=[
            pl.BlockSpec((tm, D), lambda i, f, te, xi, act: (xi[i], 0), pipeline_mode=pl.Buffered(1)),
            pl.BlockSpec((None, None, D, tf), lambda i, f, te, xi, act: (wl, te[i], 0, fidx(i, f, act))),
            pl.BlockSpec((None, None, D, tf), lambda i, f, te, xi, act: (wl, te[i], 0, fidx(i, f, act))),
            pl.BlockSpec((None, None, tf, D), lambda i, f, te, xi, act: (wl, te[i], fidx(i, f, act), 0)),
        ],
        out_specs=pl.BlockSpec((tm, D), lambda i, f, te, xi, act: (i, 0), pipeline_mode=pl.Buffered(1)),
    )
    return pl.pallas_call(
        functools.partial(_ffn_kernel, n_sub=n_sub),
        out_shape=jax.ShapeDtypeStruct((P, D), F32),
        grid_spec=grid_spec,
        compiler_params=_cparams(("arbitrary", "arbitrary")), name="ffn",
    )(tile_expert, tile_x, tile_valid, x, w_gate, w_up, w_down)


def _split3(x):
    hi = x.astype(BF16)
    r1 = x - hi.astype(F32)
    mid = r1.astype(BF16)
    lo = (r1 - mid.astype(F32)).astype(BF16)
    return hi, mid, lo


def _router_kernel(h_ref, wr_ref, tri_ref, idx_ref, gate_ref, cnt_ref, carry_s, *, tm):
    i = pl.program_id(0)

    @pl.when(i == 0)
    def _():
        carry_s[...] = jnp.zeros(carry_s.shape, F32)

    hs = _split3(_rows_2d(h_ref))
    ws = _split3(wr_ref[...])
    logits = jnp.zeros((N_EXPERTS, tm), F32)
    for a in range(3):
        for b in range(3 - a):
            logits += lax.dot_general(ws[a], hs[b], NT_DIMS, preferred_element_type=F32)
    eid = lax.broadcasted_iota(I32, (N_EXPERTS, tm), 0).astype(F32)
    none = float(N_EXPERTS)
    m1 = jnp.max(logits, axis=0, keepdims=True)
    e1 = jnp.min(jnp.where(logits == m1, eid, none), axis=0, keepdims=True)
    is1 = eid == e1
    rest = jnp.where(is1, -jnp.inf, logits)
    m2 = jnp.max(rest, axis=0, keepdims=True)
    e2 = jnp.min(jnp.where(rest == m2, eid, none), axis=0, keepdims=True)
    is2 = eid == e2
    ex = jnp.exp(m2 - m1)
    g1 = 1.0 / (1.0 + ex)
    g2 = ex / (1.0 + ex)
    sel = jnp.where(jnp.logical_or(is1, is2), 1.0, 0.0)
    before = jnp.dot(sel.astype(BF16), tri_ref[...], preferred_element_type=F32) + carry_s[...]
    r1 = jnp.sum(jnp.where(is1, before, 0.0), axis=0, keepdims=True)
    r2 = jnp.sum(jnp.where(is2, before, 0.0), axis=0, keepdims=True)
    carry_s[...] = carry_s[...] + jnp.sum(sel, axis=1, keepdims=True)
    idx_ref[0:1, :] = e1.astype(I32)
    idx_ref[1:2, :] = e2.astype(I32)
    idx_ref[2:3, :] = r1.astype(I32)
    idx_ref[3:4, :] = r2.astype(I32)
    idx_ref[4:8, :] = jnp.zeros((4, tm), I32)
    gate_ref[0:1, :] = g1
    gate_ref[1:2, :] = g2
    gate_ref[2:8, :] = jnp.zeros((6, tm), F32)
    cnt_ref[...] = jnp.broadcast_to(carry_s[...], cnt_ref.shape).astype(I32)


def router_call(h, w_router, wl, *, tm=512):
    T, D = h.shape[0], h.shape[1] * h.shape[2]
    tm = _fit_tile(T, tm)
    n = T // tm
    wr_t = jnp.swapaxes(w_router, 1, 2)
    tri = (jnp.arange(tm)[:, None] < jnp.arange(tm)[None, :]).astype(BF16)
    idx, gate, cnt = pl.pallas_call(
        functools.partial(_router_kernel, tm=tm),
        out_shape=[jax.ShapeDtypeStruct((n, 8, tm), I32),
                   jax.ShapeDtypeStruct((n, 8, tm), F32),
                   jax.ShapeDtypeStruct((N_EXPERTS, 128), I32)],
        grid=(n,),
        in_specs=[
            pl.BlockSpec((tm,) + h.shape[1:], lambda i: (i, 0, 0)),
            pl.BlockSpec((None, N_EXPERTS, D), lambda i: (wl, 0, 0)),
            pl.BlockSpec((tm, tm), lambda i: (0, 0)),
        ],
        out_specs=[
            pl.BlockSpec((None, 8, tm), lambda i: (i, 0, 0)),
            pl.BlockSpec((None, 8, tm), lambda i: (i, 0, 0)),
            pl.BlockSpec((N_EXPERTS, 128), lambda i: (0, 0)),
        ],
        scratch_shapes=[pltpu.VMEM((N_EXPERTS, 1), F32)],
        compiler_params=_cparams(("arbitrary",)), name="router",
    )(h, wr_t, tri)
    rows = lambda a, k: a[:, k, :].reshape(T)
    return (rows(idx, 0), rows(idx, 1), rows(idx, 2), rows(idx, 3),
            rows(gate, 0), rows(gate, 1), cnt[:, 0])


def _start_row_copies(src_hbm, idx_ref, base, dst, sem, n):
    def body(r, carry):
        pltpu.make_async_copy(src_hbm.at[pl.ds(idx_ref[base + r], 1)],
                              dst.at[pl.ds(r, 1)], sem).start()
        return carry
    lax.fori_loop(0, n, body, 0, unroll=8)


def _wait_row_copies(src_hbm, dst, sem, n):
    pltpu.make_async_copy(src_hbm.at[pl.ds(0, n)], dst, sem).wait()


def _gather_kernel(src_ref, nact_ref, h_hbm, o_ref, buf, sem, *, tg):
    i = pl.program_id(0)
    nact = nact_ref[0]
    slot = i % 2

    def start(blk, s):
        _start_row_copies(h_hbm, src_ref, blk * tg, buf.at[s], sem.at[s], tg)

    @pl.when(jnp.logical_and(i == 0, nact > 0))
    def _():
        start(0, 0)

    @pl.when(i + 1 < nact)
    def _():
        start(i + 1, 1 - slot)

    @pl.when(i < nact)
    def _():
        _wait_row_copies(h_hbm, buf.at[slot], sem.at[slot], tg)
        o_ref[...] = _rows_2d(buf, slot).astype(o_ref.dtype)

    @pl.when(i >= nact)
    def _():
        o_ref[...] = jnp.zeros(o_ref.shape, o_ref.dtype)


def gather_call(h, src_token, n_active_blocks, *, n_slots, tg=256):
    D = h.shape[1] * h.shape[2]
    grid_spec = pltpu.PrefetchScalarGridSpec(
        num_scalar_prefetch=2,
        grid=(n_slots // tg,),
        in_specs=[pl.BlockSpec(memory_space=pl.ANY)],
        out_specs=pl.BlockSpec((tg, D), lambda i, src, nact: (i, 0)),
        scratch_shapes=[pltpu.VMEM((2, tg) + h.shape[1:], F32), pltpu.SemaphoreType.DMA((2,))],
    )
    return pl.pallas_call(
        functools.partial(_gather_kernel, tg=tg),
        out_shape=jax.ShapeDtypeStruct((n_slots, D), BF16),
        grid_spec=grid_spec,
        compiler_params=_cparams(("arbitrary",)), name="gather",
    )(src_token, n_active_blocks, h)


def _combine_kernel(s1_ref, s2_ref, y_hbm, g1_ref, g2_ref, x_ref, gpost_ref, gtp, gts, op_ref, os_ref,
                    buf, sem, *, tc, n_tiles, n_ptiles):
    i = pl.program_id(0)
    slot = i % 2

    def start(t, s):
        _start_row_copies(y_hbm, s1_ref, t * tc, buf.at[s, 0], sem.at[s, 0], tc)
        _start_row_copies(y_hbm, s2_ref, t * tc, buf.at[s, 1], sem.at[s, 1], tc)

    @pl.when(i == 0)
    def _():
        start(0, 0)

    @pl.when(i + 1 < n_tiles)
    def _():
        start(i + 1, 1 - slot)

    _wait_row_copies(y_hbm, buf.at[slot, 0], sem.at[slot, 0], tc)
    _wait_row_copies(y_hbm, buf.at[slot, 1], sem.at[slot, 1], tc)
    f = g1_ref[...] * buf[slot, 0] + g2_ref[...] * buf[slot, 1]
    out = x_ref[...] + _mod_value(i, n_ptiles, gtp, gts) * _rms(f, gpost_ref[...])

    @pl.when(i < n_ptiles)
    def _():
        op_ref[...] = out

    @pl.when(i >= n_ptiles)
    def _():
        os_ref[...] = out


def combine_call(geom, y, slot1, slot2, g1, g2, x, g_post, modp, mods, l, gt_col):
    T, D = x.shape
    tc = geom.tm
    npt = geom.n_ptiles
    row = pl.BlockSpec((tc, D), lambda i, *_: (i, 0))
    col = pl.BlockSpec((tc, 1), lambda i, *_: (i, 0))
    grid_spec = pltpu.PrefetchScalarGridSpec(
        num_scalar_prefetch=2,
        grid=(geom.n_tiles,),
        in_specs=[pl.BlockSpec(memory_space=pl.ANY), col, col, row,
                  pl.BlockSpec((None, 1, D), lambda i, *_: (l, 0, 0))]
                 + _mod_specs(geom, D, l, gt_col),
        out_specs=[pl.BlockSpec((tc, D), lambda i, *_: (jnp.minimum(i, npt - 1), 0)),
                   pl.BlockSpec((tc, D), lambda i, *_: (jnp.maximum(i - npt, 0), 0))],
        scratch_shapes=[pltpu.VMEM((2, 2, tc, D), F32), pltpu.SemaphoreType.DMA((2, 2))],
    )
    return pl.pallas_call(
        functools.partial(_combine_kernel, tc=tc, n_tiles=geom.n_tiles, n_ptiles=geom.n_ptiles),
        out_shape=[jax.ShapeDtypeStruct((npt * tc, D), F32),
                   jax.ShapeDtypeStruct((T - npt * tc, D), F32)],
        grid_spec=grid_spec,
        compiler_params=_cparams(("arbitrary",)), name="combine",
    )(slot1, slot2, y, g1.reshape(T, 1), g2.reshape(T, 1), x, g_post, modp, mods)


def moe_dispatch_plan(e1, e2, r1, r2, counts, *, tm, n_tiles_max):
    T = e1.shape[0]
    tiles_e = (counts + tm - 1) // tm
    tile_end = jnp.cumsum(tiles_e)
    tile_off = tile_end - tiles_e
    n_active = tile_end[-1]
    slot1 = tile_off[e1] * tm + r1
    slot2 = tile_off[e2] * tm + r2
    tok = jnp.arange(T, dtype=I32)
    src = jnp.zeros((n_tiles_max * tm,), I32).at[slot1].set(tok).at[slot2].set(tok)
    t = jnp.arange(n_tiles_max, dtype=I32)
    tile_x = jnp.minimum(t, n_active - 1).astype(I32)
    tile_expert = jnp.minimum(jnp.sum(tile_end[None, :] <= tile_x[:, None], axis=1), N_EXPERTS - 1).astype(I32)
    rows_left = counts[tile_expert] - (tile_x - tile_off[tile_expert]) * tm
    tile_valid = jnp.where(t < n_active, jnp.clip(rows_left, 0, tm), 0).astype(I32)
    return slot1.astype(I32), slot2.astype(I32), src, tile_expert, tile_x, tile_valid, n_active.astype(I32)


def kernel(x_prompt, x_sample, c_prompt, c_sample, cache_k, cache_v, page_table, g_pre_mix, g_post_mix, g_pre_ffn, g_post_ffn, w_ada, b_ada, w_in, w_out, lam_q1, lam_k1, lam_q2, lam_k2, g_subln, gm_ws, gm_b, gm_norm_g, gm_norm_b, w_d_gate, w_d_up, w_d_down, w_router, w_e_gate, w_e_up, w_e_down):
    B, S, D = x_prompt.shape
    Bd, Td, _ = x_sample.shape
    L = w_in.shape[0]
    NP, NS = B * S, Bd * Td
    T = NP + NS
    H = DA_HEADS
    QKW = 2 * DA_HEAD_DIM * H
    VW = DA_VDIM * H
    GW = GM_HEADS * GM_DIM
    assert S % CHUNK == 0 and NS % CHUNK == 0 and CHUNK % Td == 0

    tm_norm = min(256, NS)
    geom = RowGeom(B, S, Bd, Td, tm_norm)
    tm_ffn = min(1024, NS)
    assert T % tm_ffn == 0

    x = jnp.concatenate([x_prompt.reshape(NP, D), x_sample.reshape(NS, D)], axis=0)
    n_c = B + Bd
    c_all = jnp.concatenate([c_prompt, c_sample, jnp.zeros((-n_c % 8, D), F32)], axis=0)
    mod = ada_call(c_all, w_ada, b_ada)
    modp = mod[:, :B].reshape(L, B, 1, 6 * D)
    mods = jnp.repeat(mod[:, B:n_c], Td, axis=1)
    SH_M, SC_M, GT_M, SH_F, SC_F, GT_F = range(6)

    g3 = lambda a: a.reshape(L, 1, a.shape[-1])
    g_pre_mix3, g_post_mix3, g_pre_ffn3, g_post_ffn3 = map(g3, (g_pre_mix, g_post_mix, g_pre_ffn, g_post_ffn))
    ng3 = gm_norm_g.reshape(L, 1, GW)
    nb3 = gm_norm_b.reshape(L, 1, GW)
    lamv = jnp.stack([lam_q1, lam_k1, lam_q2, lam_k2], axis=1)

    n_dense_tiles = T // tm_ffn
    dense_te = jnp.zeros((n_dense_tiles,), I32)
    dense_xi = jnp.arange(n_dense_tiles, dtype=I32)
    dense_act = jnp.full((n_dense_tiles,), tm_ffn, I32)

    (h,) = norm_call(geom, x, modp, mods, g_pre=g_pre_mix3, l_pre=0, sc_col=SC_M, sh_col=SH_M)

    k_p, v_p, k_s, v_s, gv_s = [], [], [], [], []
    for l in range(L):
        lam_init = 0.8 - 0.6 * math.exp(-0.3 * l)
        z, kt = proj_in_call(h, w_in, l, n_batch=B, seq=S, k_col0=QKW, k_width=QKW)
        k_p.append(jnp.transpose(kt.reshape(B, H, 2, DA_HEAD_DIM, S), (0, 4, 1, 2, 3)))
        v_p.append(z[:NP, 2 * QKW:2 * QKW + VW].reshape(B, S, H, DA_VDIM))
        k_s.append(z[NP:, QKW:2 * QKW].reshape(Bd, Td, H, 2, DA_HEAD_DIM))
        v_s.append(z[NP:, 2 * QKW:2 * QKW + VW].reshape(Bd, Td, H, DA_VDIM))

        o_att_p, o_att_s = attn_call(z, cache_k, cache_v, page_table, lamv, g_subln, l,
                                     n_batch=B, seq=S, dec_len=Td, lam_init=lam_init)

        reps = CHUNK // Td
        wmix = jnp.stack([gm_ws[l], jnp.tile(gm_ws[l][:, :Td, :Td], (1, reps, reps))])
        bmix = jnp.stack([gm_b[l].T, jnp.tile(gm_b[l][:, :Td], (1, reps)).T])
        o_gm, gvn = chunk_mlp_call(z, wmix, bmix, ng3, nb3, l, n_ptiles=NP // CHUNK, dec_len=Td,
                                   col0=2 * QKW + VW)
        gv_s.append(gvn[NP:].reshape(Bd, Td, GM_HEADS, GM_DIM))

        m = mm_call([(o_att_p, o_att_s), o_gm], w_out, l)
        is_moe = l % 2 == 1
        x, h2 = norm_call(geom, x, modp, mods, m=m, g_post=g_post_mix3, l_post=l, gt_col=GT_M,
                          g_pre=g_pre_ffn3, l_pre=l, sc_col=SC_F, sh_col=SH_F,
                          h_dtype=F32 if is_moe else BF16, h_row_major=is_moe)
        wl = l // 2
        if not is_moe:
            f = ffn_call(h2, w_d_gate[:, None], w_d_up[:, None], w_d_down[:, None], wl,
                         dense_te, dense_xi, dense_act, tm=tm_ffn)
            if l + 1 < L:
                x, h = norm_call(geom, x, modp, mods, m=f, g_post=g_post_ffn3, l_post=l, gt_col=GT_F,
                                 g_pre=g_pre_mix3, l_pre=l + 1, sc_col=SC_M, sh_col=SH_M)
            else:
                (x,) = norm_call(geom, x, modp, mods, m=f, g_post=g_post_ffn3, l_post=l, gt_col=GT_F)
        else:
            e1, e2, r1, r2, g1, g2, counts = router_call(h2, w_router, wl)
            n_tiles_max = (2 * T + N_EXPERTS * (tm_ffn - 1)) // tm_ffn
            slot1, slot2, src, te, xi, act, n_active = moe_dispatch_plan(
                e1, e2, r1, r2, counts, tm=tm_ffn, n_tiles_max=n_tiles_max)
            tg = min(256, tm_ffn)
            xs = gather_call(h2, src, (n_active * (tm_ffn // tg)).reshape(1),
                             n_slots=n_tiles_max * tm_ffn, tg=tg)
            y = ffn_call(xs, w_e_gate, w_e_up, w_e_down, wl, te, xi, act, tm=tm_ffn)
            x_halves = combine_call(geom, y, slot1, slot2, g1, g2, x, g_post_ffn3, modp, mods, l, GT_F)
            x = None
            if l + 1 < L:
                x = jnp.concatenate(x_halves, axis=0)
                (h,) = norm_call(geom, x, modp, mods, g_pre=g_pre_mix3, l_pre=l + 1, sc_col=SC_M, sh_col=SH_M)

    if x is not None:
        x_halves = (x[:NP], x[NP:])
    y_prompt = x_halves[0].reshape(B, S, D)
    y_sample = x_halves[1].reshape(Bd, Td, D)
    return (y_prompt, y_sample, jnp.stack(k_p), jnp.stack(v_p), jnp.stack(k_s), jnp.stack(v_s),
            jnp.stack(gv_s))
```
